```python
import math
import functools
import jax
import jax.numpy as jnp
from jax import lax
import numpy as np

D_MODEL = 2048
BATCH = 4
SEQ = 2048
DEPTH = 1
DEC_BATCH = 32
DEC_SEQ = 8
PAST_LEN = 16384
PAGE_SIZE = 128

N_HEADS = 8
HEAD_DIM = 128
ATTN_WIDTH = N_HEADS * HEAD_DIM
MOBA_BLOCK = 256
MOBA_TOPK = 3
ROPE_THETA = 10000.0
PROMPT_Q_CHUNK = 64
GMLP_WIDTH = D_MODEL // 2
GMLP_GROUPS = 8
GMLP_GROUP_DIM = GMLP_WIDTH // GMLP_GROUPS
CHUNK = 128
N_EXPERTS = 32
TOP_K = 4
D_EXPERT = D_MODEL
SWIGLU_ALPHA = 1.702
SWIGLU_LIMIT = 7.0
MOE_BLOCK = 128
EPS = 1e-6
IN_WIDTH = 3 * ATTN_WIDTH + 2 * GMLP_WIDTH + 2 * D_MODEL

kernel_name = 'hybrid_gmlp_moba_moe_step'


def _rmsnorm(x, g):
    xf = x.astype(jnp.float32)
    y = xf * lax.rsqrt(jnp.mean(xf * xf, axis=-1, keepdims=True) + EPS)
    return (y * g.astype(jnp.float32)).astype(x.dtype)


def _layernorm(x, g, b):
    xf = x.astype(jnp.float32)
    mu = jnp.mean(xf, axis=-1, keepdims=True)
    var = jnp.mean(jnp.square(xf - mu), axis=-1, keepdims=True)
    y = (xf - mu) * lax.rsqrt(var + EPS) * g.astype(jnp.float32) + b.astype(jnp.float32)
    return y.astype(x.dtype)


def _rope(x, pos):
    half = HEAD_DIM // 2
    inv_freq = ROPE_THETA ** (-jnp.arange(half, dtype=jnp.float32) / half)
    ang = pos.astype(jnp.float32)[:, None] * inv_freq[None, :]
    cos = jnp.cos(ang)[:, None, :]
    sin = jnp.sin(ang)[:, None, :]
    xf = x.astype(jnp.float32)
    x1, x2 = xf[..., :half], xf[..., half:]
    return jnp.concatenate([x1 * cos - x2 * sin, x2 * cos + x1 * sin], axis=-1).astype(x.dtype)


def _adaln(c, w_ada, b_ada):
    mod = jax.nn.silu(c) @ w_ada + b_ada
    return jnp.split(mod[:, None, :], 6, axis=-1)


def _chunk_gmlp(u, gv, w_s, b_s):
    b, s, _ = u.shape
    sp = -(-s // CHUNK) * CHUNK
    vp = jnp.pad(gv, ((0, 0), (0, sp - s), (0, 0)))
    vc = vp.reshape(b, sp // CHUNK, CHUNK, GMLP_GROUPS, GMLP_GROUP_DIM)
    causal = jnp.tril(jnp.ones((CHUNK, CHUNK), dtype=bool))
    ws = jnp.where(causal[None], w_s, jnp.zeros((), w_s.dtype))
    mixed = jnp.einsum('gts,bnsgc->bntgc', ws, vc) + jnp.transpose(b_s)[:, :, None]
    mixed = mixed.reshape(b, sp, GMLP_WIDTH)[:, :s]
    return u * mixed


def _moba_blocks(k_all, v_all):
    l = k_all.shape[0]
    nb = -(-l // MOBA_BLOCK)
    pad = ((0, nb * MOBA_BLOCK - l), (0, 0), (0, 0))
    kb = jnp.pad(k_all, pad).reshape(nb, MOBA_BLOCK, N_HEADS, HEAD_DIM)
    vb = jnp.pad(v_all, pad).reshape(nb, MOBA_BLOCK, N_HEADS, HEAD_DIM)
    means = jnp.mean(kb.astype(jnp.float32), axis=1)
    return kb, vb, means


def _moba_core(q, pos, kb, vb, means):
    nb = kb.shape[0]
    k_sel = min(MOBA_TOPK, nb)
    own = pos // MOBA_BLOCK
    gate = jnp.einsum('thd,nhd->thn', q.astype(jnp.float32), means)
    cand = jnp.arange(nb)[None, None, :] < own[:, None, None]
    gate = jnp.where(cand, gate, -jnp.inf)
    _, top_idx = lax.top_k(gate, k_sel)
    sel = jnp.concatenate([top_idx, jnp.broadcast_to(own[:, None, None], (q.shape[0], N_HEADS, 1))], axis=-1)
    kbh = jnp.transpose(kb, (2, 0, 1, 3))
    vbh = jnp.transpose(vb, (2, 0, 1, 3))
    h_idx = jnp.arange(N_HEADS)[None, :, None]
    kg = kbh[h_idx, sel]
    vg = vbh[h_idx, sel]
    key_pos = sel[..., None] * MOBA_BLOCK + jnp.arange(MOBA_BLOCK)
    slot = jnp.arange(k_sel + 1)
    slot_ok = (slot[None, :] < own[:, None]) | (slot[None, :] == k_sel)
    valid = slot_ok[:, None, :, None] & (key_pos <= pos[:, None, None, None])
    s = jnp.einsum('thd,thjpd->thjp', q, kg).astype(jnp.float32) * (HEAD_DIM ** -0.5)
    s = jnp.where(valid, s, -jnp.inf)
    t = q.shape[0]
    w = jax.nn.softmax(s.reshape(t, N_HEADS, -1), axis=-1).reshape(s.shape)
    return jnp.einsum('thjp,thjpd->thd', w.astype(vg.dtype), vg)


def _moba_prompt(q, k, v, pos):
    b, s = q.shape[:2]
    kb, vb, means = jax.vmap(_moba_blocks)(k, v)
    qc = math.gcd(s, PROMPT_Q_CHUNK)
    nc = s // qc
    q_chunks = q.reshape(b * nc, qc, N_HEADS, HEAD_DIM)
    pos_chunks = jnp.tile(pos.reshape(nc, qc), (b, 1))
    b_idx = jnp.repeat(jnp.arange(b, dtype=jnp.int32), nc)

    def step(args):
        qi, pi, bi = args
        return _moba_core(qi, pi, kb[bi], vb[bi], means[bi])

    out = lax.map(step, (q_chunks, pos_chunks, b_idx))
    return out.reshape(b, s, N_HEADS, HEAD_DIM)


def _moba_sample(q, k, v, pos, cache_k, cache_v, page_table):
    def step(args):
        qi, ki, vi, pt = args
        past_k = cache_k[pt].reshape(-1, N_HEADS, HEAD_DIM)
        past_v = cache_v[pt].reshape(-1, N_HEADS, HEAD_DIM)
        k_all = jnp.concatenate([past_k, ki.astype(past_k.dtype)], axis=0)
        v_all = jnp.concatenate([past_v, vi.astype(past_v.dtype)], axis=0)
        kb, vb, means = _moba_blocks(k_all, v_all)
        return _moba_core(qi, pos, kb, vb, means)

    return lax.map(step, (q, k, v, page_table))


def _expert_ffn(xb, w1, b1, w2, b2):
    hdn = xb @ w1 + b1
    glu, lin = hdn[..., :D_EXPERT], hdn[..., D_EXPERT:]
    glu = jnp.minimum(glu, SWIGLU_LIMIT)
    lin = jnp.clip(lin, -SWIGLU_LIMIT, SWIGLU_LIMIT)
    act = glu * jax.nn.sigmoid(SWIGLU_ALPHA * glu) * (lin + 1.0)
    return act @ w2 + b2


def _moe(x, w_r, b_r, w1, b1, w2, b2):
    n, d = x.shape
    a = n * TOP_K
    logits = (x @ w_r + b_r).astype(jnp.float32)
    top_val, top_idx = lax.top_k(logits, TOP_K)
    gates = jax.nn.softmax(top_val, axis=-1)
    e_flat = top_idx.reshape(-1)
    order = jnp.argsort(e_flat)
    e_sorted = e_flat[order]
    tok_sorted = order // TOP_K
    counts = jnp.bincount(e_flat, length=N_EXPERTS)
    padded = (counts + MOE_BLOCK - 1) // MOE_BLOCK * MOE_BLOCK
    pad_end = jnp.cumsum(padded)
    pad_start = pad_end - padded
    start = jnp.cumsum(counts) - counts
    dest = pad_start[e_sorted] + jnp.arange(a) - start[e_sorted]
    n_blocks = (a + N_EXPERTS * (MOE_BLOCK - 1) + MOE_BLOCK - 1) // MOE_BLOCK
    p = n_blocks * MOE_BLOCK
    xbuf = jnp.zeros((p, d), x.dtype).at[dest].set(x[tok_sorted])
    block_e = jnp.minimum(jnp.searchsorted(pad_end, jnp.arange(n_blocks) * MOE_BLOCK, side='right'), N_EXPERTS - 1)

    def step(args):
        xb, e = args
        return _expert_ffn(xb, w1[e], b1[e], w2[e], b2[e])

    ybuf = lax.map(step, (xbuf.reshape(n_blocks, MOE_BLOCK, d), block_e)).reshape(p, d)
    y_assign = jnp.zeros((a, d), x.dtype).at[order].set(ybuf[dest]).reshape(n, TOP_K, d)
    return jnp.einsum('nk,nkd->nd', gates.astype(x.dtype), y_assign)


def _token_mixer(h, pos, attend, w_in, b_in, g_q, g_k, ln_g, ln_b, w_s, b_s, w_a, w_b, w_o):
    b, s, _ = h.shape
    z = h @ w_in + b_in
    offs = [ATTN_WIDTH, 2 * ATTN_WIDTH, 3 * ATTN_WIDTH,
            3 * ATTN_WIDTH + GMLP_WIDTH, 3 * ATTN_WIDTH + 2 * GMLP_WIDTH,
            3 * ATTN_WIDTH + 2 * GMLP_WIDTH + D_MODEL]
    q, k, v, u, gv, ga, gb = jnp.split(z, offs, axis=-1)
    q = _rope(_rmsnorm(q.reshape(b, s, N_HEADS, HEAD_DIM), g_q), pos)
    k = _rope(_rmsnorm(k.reshape(b, s, N_HEADS, HEAD_DIM), g_k), pos)
    v = v.reshape(b, s, N_HEADS, HEAD_DIM)
    u = jax.nn.gelu(u, approximate=False)
    gv = _layernorm(jax.nn.gelu(gv, approximate=False), ln_g, ln_b)
    y_a = _chunk_gmlp(u, gv, w_s, b_s)
    y_b = attend(q, k, v).reshape(b, s, ATTN_WIDTH)
    merged = jax.nn.sigmoid(ga) * (y_a @ w_a) + jax.nn.sigmoid(gb) * (y_b @ w_b)
    return merged @ w_o, k, v, gv


def _layer(x, c, pos, attend, w_ada, b_ada, g1, g2, w_in, b_in, g_q, g_k, ln_g, ln_b,
           w_s, b_s, w_a, w_b, w_o, w_r, b_r, w1, b1, w2, b2):
    sh1, sc1, gt1, sh2, sc2, gt2 = _adaln(c, w_ada, b_ada)
    h = _rmsnorm(x, g1) * (1.0 + sc1) + sh1
    mix, k, v, gv = _token_mixer(h, pos, attend, w_in, b_in, g_q, g_k, ln_g, ln_b, w_s, b_s, w_a, w_b, w_o)
    x = x + gt1 * mix
    h2 = _rmsnorm(x, g2) * (1.0 + sc2) + sh2
    ff = _moe(h2.reshape(-1, D_MODEL), w_r, b_r, w1, b1, w2, b2).reshape(x.shape)
    x = x + gt2 * ff
    return x, k, v, gv


def setup_inputs(seed: int = 0) -> dict:
    key = jax.random.key(seed)
    ks = jax.random.split(key, 32)
    n_pages = PAST_LEN // PAGE_SIZE
    n_phys = (DEC_BATCH * n_pages * 5) // 4

    def nrm(k, shape, scale):
        return jax.random.normal(k, shape, jnp.float32) * scale

    page_table = jax.random.permutation(ks[4], n_phys)[:DEC_BATCH * n_pages].reshape(DEC_BATCH, n_pages).astype(jnp.int32)
    return {
        'x_prompt': nrm(ks[0], (BATCH, SEQ, D_MODEL), 1.0),
        'x_sample': nrm(ks[1], (DEC_BATCH, DEC_SEQ, D_MODEL), 1.0),
        'cache_k': nrm(ks[2], (DEPTH, n_phys, PAGE_SIZE, N_HEADS, HEAD_DIM), 1.0),
        'cache_v': nrm(ks[3], (DEPTH, n_phys, PAGE_SIZE, N_HEADS, HEAD_DIM), 1.0),
        'page_table': page_table,
        'c_prompt': nrm(ks[5], (BATCH, D_MODEL), 1.0),
        'c_sample': nrm(ks[6], (DEC_BATCH, D_MODEL), 1.0),
        'w_ada': nrm(ks[7], (DEPTH, D_MODEL, 6 * D_MODEL), 0.5 * D_MODEL ** -0.5),
        'b_ada': nrm(ks[8], (DEPTH, 6 * D_MODEL), 0.02),
        'g_norm1': 1.0 + nrm(ks[9], (DEPTH, D_MODEL), 0.05),
        'g_norm2': 1.0 + nrm(ks[10], (DEPTH, D_MODEL), 0.05),
        'w_in': nrm(ks[11], (DEPTH, D_MODEL, IN_WIDTH), D_MODEL ** -0.5),
        'b_in': nrm(ks[12], (DEPTH, IN_WIDTH), 0.02),
        'g_q': 1.0 + nrm(ks[13], (DEPTH, HEAD_DIM), 0.05),
        'g_k': 1.0 + nrm(ks[14], (DEPTH, HEAD_DIM), 0.05),
        'gmlp_ln_g': 1.0 + nrm(ks[15], (DEPTH, GMLP_WIDTH), 0.05),
        'gmlp_ln_b': nrm(ks[16], (DEPTH, GMLP_WIDTH), 0.02),
        'w_spatial': nrm(ks[17], (DEPTH, GMLP_GROUPS, CHUNK, CHUNK), CHUNK ** -0.5),
        'b_spatial': 1.0 + nrm(ks[18], (DEPTH, GMLP_GROUPS, CHUNK), 0.05),
        'w_branch_a': nrm(ks[19], (DEPTH, GMLP_WIDTH, D_MODEL), GMLP_WIDTH ** -0.5),
        'w_branch_b': nrm(ks[20], (DEPTH, ATTN_WIDTH, D_MODEL), ATTN_WIDTH ** -0.5),
        'w_out': nrm(ks[21], (DEPTH, D_MODEL, D_MODEL), D_MODEL ** -0.5),
        'w_router': nrm(ks[22], (DEPTH, D_MODEL, N_EXPERTS), D_MODEL ** -0.5),
        'b_router': nrm(ks[23], (DEPTH, N_EXPERTS), 0.01),
        'w_expert_in': nrm(ks[24], (DEPTH, N_EXPERTS, D_MODEL, 2 * D_EXPERT), D_MODEL ** -0.5),
        'b_expert_in': nrm(ks[25], (DEPTH, N_EXPERTS, 2 * D_EXPERT), 0.01),
        'w_expert_out': nrm(ks[26], (DEPTH, N_EXPERTS, D_EXPERT, D_MODEL), D_EXPERT ** -0.5),
        'b_expert_out': nrm(ks[27], (DEPTH, N_EXPERTS, D_MODEL), 0.01),
    }


def reference(x_prompt, x_sample, cache_k, cache_v, page_table, c_prompt, c_sample,
              w_ada, b_ada, g_norm1, g_norm2, w_in, b_in, g_q, g_k, gmlp_ln_g, gmlp_ln_b,
              w_spatial, b_spatial, w_branch_a, w_branch_b, w_out, w_router, b_router,
              w_expert_in, b_expert_in, w_expert_out, b_expert_out):
    past_len = page_table.shape[1] * cache_k.shape[2]
    pos_p = jnp.arange(x_prompt.shape[1], dtype=jnp.int32)
    pos_s = past_len + jnp.arange(x_sample.shape[1], dtype=jnp.int32)
    xp, xs = x_prompt, x_sample
    kp_l, vp_l, ks_l, vs_l, gvs_l = [], [], [], [], []
    for l in range(DEPTH):
        lw = (w_ada[l], b_ada[l], g_norm1[l], g_norm2[l], w_in[l], b_in[l], g_q[l], g_k[l],
              gmlp_ln_g[l], gmlp_ln_b[l], w_spatial[l], b_spatial[l], w_branch_a[l], w_branch_b[l],
              w_out[l], w_router[l], b_router[l], w_expert_in[l], b_expert_in[l],
              w_expert_out[l], b_expert_out[l])
        attend_p = functools.partial(_moba_prompt, pos=pos_p)
        attend_s = functools.partial(_moba_sample, pos=pos_s, cache_k=cache_k[l],
                                     cache_v=cache_v[l], page_table=page_table)
        xp, kp, vp, _ = _layer(xp, c_prompt, pos_p, attend_p, *lw)
        xs, ks_, vs_, gvs = _layer(xs, c_sample, pos_s, attend_s, *lw)
        kp_l.append(kp)
        vp_l.append(vp)
        ks_l.append(ks_)
        vs_l.append(vs_)
        gvs_l.append(gvs)
    return (xp, xs, jnp.stack(kp_l), jnp.stack(vp_l), jnp.stack(ks_l), jnp.stack(vs_l), jnp.stack(gvs_l))
```

```python
import functools
import math

import jax
import jax.numpy as jnp
from jax import lax
from jax.experimental import pallas as pl
from jax.experimental.pallas import tpu as pltpu

F32 = jnp.float32
BF16 = jnp.bfloat16
I32 = jnp.int32
HIGHEST = lax.Precision.HIGHEST

N_HEADS = 8
HEAD_DIM = 128
ATTN_WIDTH = N_HEADS * HEAD_DIM
MOBA_BLOCK = 256
MOBA_TOPK = 3
ROPE_THETA = 10000.0
GMLP_GROUPS = 8
GMLP_GROUP_DIM = 128
CHUNK = 128
N_EXPERTS = 32
TOP_K = 4
SWIGLU_ALPHA = 1.702
SWIGLU_LIMIT = 7.0
EPS = 1e-6
NEG_BIG = -1e30

LANES = 128
MOE_ROWS = 256
PAGES_PER_STEP = 8
VMEM_LIMIT = 56 * 1024 * 1024


def _cparams(sem):
    return pltpu.CompilerParams(dimension_semantics=sem, vmem_limit_bytes=VMEM_LIMIT)


def _ada_kernel(c_ref, w_ref, b_ref, o_ref):
    c = c_ref[...]
    s = (c * jax.nn.sigmoid(c)).astype(BF16)
    o_ref[...] = jnp.dot(s, w_ref[...].astype(BF16), preferred_element_type=F32) + b_ref[...]


def _adaln(c, w_ada, b_ada):
    m, d = c.shape
    n = w_ada.shape[1]
    tn = 1024
    return pl.pallas_call(
        _ada_kernel,
        grid=(n // tn,),
        in_specs=[pl.BlockSpec((m, d), lambda j: (0, 0)),
                  pl.BlockSpec((d, tn), lambda j: (0, j)),
                  pl.BlockSpec((1, tn), lambda j: (0, j))],
        out_specs=pl.BlockSpec((m, tn), lambda j: (0, j)),
        out_shape=jax.ShapeDtypeStruct((m, n), F32),
        compiler_params=_cparams(("arbitrary",)),
        name="adaln",
    )(c, w_ada, b_ada.reshape(1, n))


def _gelu(x):
    return 0.5 * x * (1.0 + lax.erf(x * (1.0 / math.sqrt(2.0))))


def _in_kernel(x_ref, sc_ref, sh_ref, g1_ref, w_ref, b_ref, gq_ref, gk_ref, cos_ref, sin_ref, lng_ref, lnb_ref,
               q_ref, k_ref, v_ref, u_ref, gv_ref, sga_ref, sgb_ref, h_scr):
    j = pl.program_id(1)

    @pl.when(j == 0)
    def _():
        x = x_ref[...]
        r = lax.rsqrt(jnp.mean(x * x, axis=-1, keepdims=True) + EPS)
        h = (x * r * g1_ref[...]) * (1.0 + sc_ref[0]) + sh_ref[0]
        h_scr[...] = h.astype(BF16)

    z = jnp.dot(h_scr[...], w_ref[...], preferred_element_type=F32) + b_ref[...]

    def qk_post(g_ref, o_ref):
        cos = cos_ref[...]
        sin = sin_ref[...]
        g = g_ref[...]
        for hh in range(N_HEADS):
            zh = z[:, hh * HEAD_DIM:(hh + 1) * HEAD_DIM]
            r = lax.rsqrt(jnp.mean(zh * zh, axis=-1, keepdims=True) + EPS)
            y = zh * r * g
            o_ref[:, hh * HEAD_DIM:(hh + 1) * HEAD_DIM] = y * cos + pltpu.roll(y, HEAD_DIM // 2, 1) * sin

    @pl.when(j == 0)
    def _():
        qk_post(gq_ref, q_ref)

    @pl.when(j == 1)
    def _():
        qk_post(gk_ref, k_ref)

    @pl.when(j == 2)
    def _():
        v_ref[...] = z

    @pl.when(j == 3)
    def _():
        u_ref[...] = _gelu(z).astype(u_ref.dtype)

    @pl.when(j == 4)
    def _():
        a = _gelu(z)
        mu = jnp.mean(a, axis=-1, keepdims=True)
        d = a - mu
        var = jnp.mean(d * d, axis=-1, keepdims=True)
        gv_ref[...] = d * lax.rsqrt(var + EPS) * lng_ref[...] + lnb_ref[...]

    @pl.when((j == 5) | (j == 6))
    def _():
        sga_ref[...] = jax.nn.sigmoid(z).astype(sga_ref.dtype)

    @pl.when(j >= 7)
    def _():
        sgb_ref[...] = jax.nn.sigmoid(z).astype(sgb_ref.dtype)


def _in_proj(x, sc, sh, g1, w_in, b_in, g_q, g_k, cos, sin, ln_g, ln_b, tm):
    m, d = x.shape
    n = w_in.shape[1]
    tn = 1024
    assert n == 9 * tn and m % tm == 0
    nb, r, _ = sc.shape
    tiles_per_mod = m // (nb * tm)
    tiles_per_tab = cos.shape[0] // tm
    row = lambda i, j: (i, 0)
    mod_spec = pl.BlockSpec((1, r, d), lambda i, j: (i // tiles_per_mod, 0, 0))
    vec_d = pl.BlockSpec((1, d), lambda i, j: (0, 0))
    vec_h = pl.BlockSpec((1, HEAD_DIM), lambda i, j: (0, 0))
    vec_n = pl.BlockSpec((1, tn), lambda i, j: (0, 0))
    tab = pl.BlockSpec((tm, HEAD_DIM), lambda i, j: (i % tiles_per_tab, 0))
    outs = pl.pallas_call(
        _in_kernel,
        grid=(m // tm, n // tn),
        in_specs=[pl.BlockSpec((tm, d), row), mod_spec, mod_spec, vec_d,
                  pl.BlockSpec((d, tn), lambda i, j: (0, j)),
                  pl.BlockSpec((1, tn), lambda i, j: (0, j)),
                  vec_h, vec_h, tab, tab, vec_n, vec_n],
        out_specs=[pl.BlockSpec((tm, tn), row), pl.BlockSpec((tm, tn), row), pl.BlockSpec((tm, tn), row),
                   pl.BlockSpec((tm, tn), row), pl.BlockSpec((tm, tn), row),
                   pl.BlockSpec((tm, tn), lambda i, j: (i, jnp.clip(j - 5, 0, 1))),
                   pl.BlockSpec((tm, tn), lambda i, j: (i, jnp.clip(j - 7, 0, 1)))],
        out_shape=[jax.ShapeDtypeStruct((m, tn), F32), jax.ShapeDtypeStruct((m, tn), F32),
                   jax.ShapeDtypeStruct((m, tn), F32), jax.ShapeDtypeStruct((m, tn), BF16),
                   jax.ShapeDtypeStruct((m, tn), F32), jax.ShapeDtypeStruct((m, 2 * tn), BF16),
                   jax.ShapeDtypeStruct((m, 2 * tn), BF16)],
        scratch_shapes=[pltpu.VMEM((tm, d), BF16)],
        compiler_params=_cparams(("arbitrary", "arbitrary")),
        name="in_proj",
    )(x, sc, sh, g1.reshape(1, d), w_in, b_in.reshape(1, n), g_q.reshape(1, HEAD_DIM), g_k.reshape(1, HEAD_DIM),
      cos, sin, ln_g.reshape(1, tn), ln_b.reshape(1, tn))
    return outs


def _moba_prompt_kernel(q_ref, k_ref, v_ref, o_ref):
    s_len = q_ref.shape[1]
    nb = s_len // MOBA_BLOCK
    blk = MOBA_BLOCK
    q = q_ref[0]
    k = k_ref[0]
    means = jnp.concatenate(
        [jnp.mean(k[n * blk:(n + 1) * blk], axis=0, keepdims=True) for n in range(nb)], axis=0)
    gate = lax.dot_general(means, q, (((1,), (1,)), ((), ())), precision=HIGHEST,
                           preferred_element_type=F32)
    n_iota = lax.broadcasted_iota(I32, (nb, s_len), 0)
    own = lax.broadcasted_iota(I32, (nb, s_len), 1) // blk
    cnt = jnp.zeros((nb, s_len), I32)
    for m in range(nb):
        gm = gate[m:m + 1, :]
        beats = (m < own) & ((gm > gate) | ((gm == gate) & (m < n_iota)))
        cnt = cnt + beats.astype(I32)
    sel = ((n_iota < own) & (cnt < MOBA_TOPK)).astype(F32)
    sel = jnp.concatenate([sel, jnp.zeros((LANES - nb, s_len), F32)], axis=0)

    qb = q.astype(BF16)
    kb = k.astype(BF16)
    vb = v_ref[0].astype(BF16)
    scale = HEAD_DIM ** -0.5
    row = lax.broadcasted_iota(I32, (blk, blk), 0)
    col = lax.broadcasted_iota(I32, (blk, blk), 1)
    for i in range(nb):
        w = (i + 1) * blk
        s = lax.dot_general(qb[i * blk:(i + 1) * blk], kb[:w], (((1,), (1,)), ((), ())),
                            preferred_element_type=F32) * scale
        sel_t = sel[:, i * blk:(i + 1) * blk].T
        parts = [jnp.broadcast_to(sel_t[:, jj:jj + 1], (blk, blk)) for jj in range(i)]
        parts.append(jnp.where(col <= row, 1.0, 0.0))
        allowed = jnp.concatenate(parts, axis=1) if i else parts[0]
        s = jnp.where(allowed > 0.0, s, NEG_BIG)
        mx = jnp.max(s, axis=1, keepdims=True)
        p = jnp.exp(s - mx)
        l = jnp.sum(p, axis=1, keepdims=True)
        o = jnp.dot(p.astype(BF16), vb[:w], preferred_element_type=F32) / l
        o_ref[0, i * blk:(i + 1) * blk, :] = o.astype(o_ref.dtype)


def _moba_prompt(q, k, v):
    b, s, _ = q.shape
    assert s % MOBA_BLOCK == 0
    spec = pl.BlockSpec((1, s, HEAD_DIM), lambda bi, hi: (bi, 0, hi))
    return pl.pallas_call(
        _moba_prompt_kernel,
        grid=(b, N_HEADS),
        in_specs=[spec, spec, spec],
        out_specs=spec,
        out_shape=jax.ShapeDtypeStruct(q.shape, BF16),
        compiler_params=_cparams(("arbitrary", "arbitrary")),
        name="moba_prompt",
    )(q, k, v)


def _moba_sample_kernel(pt_ref, q_ref, kn_ref, vn_ref, kc_hbm, vc_hbm, o_ref,
                        buf, sem, qbd_scr, sc_scr, p_scr, means_scr, pown_scr, l_scr, acc_scr,
                        *, n_pages, n_req):
    pg = PAGES_PER_STEP
    ns = n_pages // pg
    nblk = n_pages // 2
    rows = N_HEADS * kn_ref.shape[1]
    n_tok = kn_ref.shape[1]
    r = pl.program_id(0)
    c = pl.program_id(1)
    step = r * (2 * ns) + c
    total = n_req * 2 * ns
    slot = step % 2
    scale = HEAD_DIM ** -0.5

    def page_copy(st, sl, p):
        r_ = st // (2 * ns)
        c_ = st % (2 * ns)
        page = pt_ref[r_ * n_pages + (c_ % ns) * pg + p]
        return c_ < ns, (pltpu.make_async_copy(kc_hbm.at[page], buf.at[sl, p], sem.at[sl, p]),
                         pltpu.make_async_copy(vc_hbm.at[page], buf.at[sl, p], sem.at[sl, p]))

    def fetch(st, sl, wait):
        for p in range(pg):
            is_k, (ck, cv) = page_copy(st, sl, p)

            @pl.when(is_k)
            def _():
                ck.wait() if wait else ck.start()

            @pl.when(jnp.logical_not(is_k))
            def _():
                cv.wait() if wait else cv.start()

    @pl.when(step == 0)
    def _():
        fetch(step, slot, False)

    @pl.when(step + 1 < total)
    def _():
        fetch(step + 1, 1 - slot, False)

    fetch(step, slot, True)

    @pl.when(c == 0)
    def _():
        q8 = q_ref[0]
        qrep = jnp.concatenate([q8] * N_HEADS, axis=0)
        rh = lax.broadcasted_iota(I32, qrep.shape, 0) // n_tok
        ch = lax.broadcasted_iota(I32, qrep.shape, 1) // HEAD_DIM
        qbd_scr[...] = jnp.where(rh == ch, qrep, 0.0)

    @pl.when(c < ns)
    def _():
        qbd = qbd_scr[...].astype(BF16)
        prev = None
        for p in range(pg):
            kp = buf[slot, p]
            s = lax.dot_general(qbd, kp.astype(BF16), (((1,), (1,)), ((), ())), preferred_element_type=F32)
            b_idx = (c * pg + p) // 2
            half = p % 2
            sc_scr[b_idx, :, half * LANES:(half + 1) * LANES] = s
            psum = jnp.sum(kp, axis=0, keepdims=True)
            if half == 0:
                prev = psum
            else:
                means_scr[pl.ds(b_idx, 1), :] = (prev + psum) * (1.0 / MOBA_BLOCK)

    @pl.when(c == ns - 1)
    def _():
        qbd = qbd_scr[...]
        gate = lax.dot_general(qbd, means_scr[...], (((1,), (1,)), ((), ())), precision=HIGHEST,
                               preferred_element_type=F32)
        lane = lax.broadcasted_iota(I32, gate.shape, 1).astype(F32)
        g = gate
        sel = jnp.zeros(gate.shape, F32)
        for _ in range(MOBA_TOPK):
            mx = jnp.max(g, axis=1, keepdims=True)
            idx = jnp.min(jnp.where(g == mx, lane, float(nblk)), axis=1, keepdims=True)
            pick = lane == idx
            sel = jnp.where(pick, 1.0, sel)
            g = jnp.where(pick, -jnp.inf, g)

        kn = jnp.concatenate([kn_ref[0], jnp.zeros((LANES - n_tok, kn_ref.shape[2]), F32)], axis=0)
        s_own = lax.dot_general(qbd.astype(BF16), kn.astype(BF16), (((1,), (1,)), ((), ())),
                                preferred_element_type=F32) * scale
        t_q = lax.broadcasted_iota(I32, s_own.shape, 0) % n_tok
        t_k = lax.broadcasted_iota(I32, s_own.shape, 1)
        s_own = jnp.where(t_k <= t_q, s_own, NEG_BIG)

        def sel_col(b_idx):
            return jnp.sum(jnp.where(lane == jnp.asarray(b_idx, I32).astype(F32), sel, 0.0), axis=1, keepdims=True) > 0.0

        def max_body(b_idx, m):
            bm = jnp.max(sc_scr[b_idx], axis=1, keepdims=True) * scale
            return jnp.maximum(m, jnp.where(sel_col(b_idx), bm, NEG_BIG))

        m = lax.fori_loop(0, nblk, max_body, jnp.max(s_own, axis=1, keepdims=True))

        def exp_body(b_idx, l):
            p = jnp.where(sel_col(b_idx), jnp.exp(sc_scr[b_idx] * scale - m), 0.0)
            p_scr[b_idx] = p.astype(BF16)
            return l + jnp.sum(p, axis=1, keepdims=True)

        p_own = jnp.exp(s_own - m)
        l = lax.fori_loop(0, nblk, exp_body, jnp.sum(p_own, axis=1, keepdims=True))
        pown_scr[...] = p_own.astype(BF16)
        l_scr[...] = jnp.broadcast_to(l, l_scr.shape)

    @pl.when(c == ns)
    def _():
        vn = jnp.concatenate([vn_ref[0], jnp.zeros((LANES - n_tok, vn_ref.shape[2]), F32)], axis=0)
        acc_scr[...] = jnp.dot(pown_scr[...], vn.astype(BF16), preferred_element_type=F32)

    @pl.when(c >= ns)
    def _():
        acc = acc_scr[...]
        for p in range(pg):
            vp = buf[slot, p].astype(BF16)
            b_idx = ((c - ns) * pg + p) // 2
            half = p % 2
            acc = acc + jnp.dot(p_scr[b_idx, :, half * LANES:(half + 1) * LANES], vp, preferred_element_type=F32)
        acc_scr[...] = acc

    @pl.when(c == 2 * ns - 1)
    def _():
        o = acc_scr[...] / l_scr[:, 0:1]
        for hh in range(N_HEADS):
            o_ref[0, :, hh * HEAD_DIM:(hh + 1) * HEAD_DIM] = o[hh * n_tok:(hh + 1) * n_tok,
                                                               hh * HEAD_DIM:(hh + 1) * HEAD_DIM].astype(o_ref.dtype)


def _moba_sample(q, k_new, v_new, cache_k, cache_v, page_table):
    n_req, n_tok, width = q.shape
    n_pages = page_table.shape[1]
    page = cache_k.shape[1]
    assert page == LANES and 2 * page == MOBA_BLOCK and n_pages % (2 * PAGES_PER_STEP) == 0
    assert n_tok <= LANES and (N_HEADS * n_tok) % 8 == 0
    ns = n_pages // PAGES_PER_STEP
    nblk = n_pages // 2
    rows = N_HEADS * n_tok
    tok_spec = pl.BlockSpec((1, n_tok, width), lambda r, c, pt: (r, 0, 0))
    kern = functools.partial(_moba_sample_kernel, n_pages=n_pages, n_req=n_req)
    return pl.pallas_call(
        kern,
        grid_spec=pltpu.PrefetchScalarGridSpec(
            num_scalar_prefetch=1,
            grid=(n_req, 2 * ns),
            in_specs=[tok_spec, tok_spec, tok_spec,
                      pl.BlockSpec(memory_space=pl.ANY), pl.BlockSpec(memory_space=pl.ANY)],
            out_specs=tok_spec,
            scratch_shapes=[pltpu.VMEM((2, PAGES_PER_STEP, page, width), F32),
                            pltpu.SemaphoreType.DMA((2, PAGES_PER_STEP)),
                            pltpu.VMEM((rows, width), F32),
                            pltpu.VMEM((nblk, rows, MOBA_BLOCK), F32),
                            pltpu.VMEM((nblk, rows, MOBA_BLOCK), BF16),
                            pltpu.VMEM((nblk, width), F32),
                            pltpu.VMEM((rows, LANES), BF16),
                            pltpu.VMEM((rows, LANES), F32),
                            pltpu.VMEM((rows, width), F32)]),
        out_shape=jax.ShapeDtypeStruct(q.shape, BF16),
        compiler_params=_cparams(("arbitrary", "arbitrary")),
        name="moba_sample",
    )(page_table.reshape(-1), q, k_new, v_new, cache_k, cache_v)


def _mix_out_kernel(u_ref, gv_ref, sga_ref, sgb_ref, yb_ref, x_ref, gt1_ref, sc2_ref, sh2_ref, g2_ref,
                    ws_ref, bs_ref, wa_ref, wb_ref, wo_ref, wr_ref, br_ref,
                    x1_ref, h2_ref, idx_ref, gate_ref, ya_scr):
    tm = u_ref.shape[0]
    cs = ws_ref.shape[1]
    row = lax.broadcasted_iota(I32, (cs, cs), 0)
    col = lax.broadcasted_iota(I32, (cs, cs), 1)
    causal = col <= row
    bs = bs_ref[...]
    for g in range(GMLP_GROUPS):
        ws = jnp.where(causal, ws_ref[g], 0.0).astype(BF16)
        gs = slice(g * GMLP_GROUP_DIM, (g + 1) * GMLP_GROUP_DIM)
        for ci in range(tm // cs):
            rs = slice(ci * cs, (ci + 1) * cs)
            mixed = jnp.dot(ws, gv_ref[rs, gs].astype(BF16), preferred_element_type=F32) + bs[:, g:g + 1]
            ya_scr[rs, gs] = (u_ref[rs, gs].astype(F32) * mixed).astype(BF16)

    a = jnp.dot(ya_scr[...], wa_ref[...], preferred_element_type=F32)
    b = jnp.dot(yb_ref[...], wb_ref[...], preferred_element_type=F32)
    merged = sga_ref[...].astype(F32) * a + sgb_ref[...].astype(F32) * b
    mix = jnp.dot(merged.astype(BF16), wo_ref[...], preferred_element_type=F32)
    x1 = x_ref[...] + gt1_ref[0] * mix
    x1_ref[...] = x1
    r = lax.rsqrt(jnp.mean(x1 * x1, axis=-1, keepdims=True) + EPS)
    h2 = (x1 * r * g2_ref[...]) * (1.0 + sc2_ref[0]) + sh2_ref[0]
    h2_ref[...] = h2

    logits = jnp.dot(h2, wr_ref[...], precision=HIGHEST, preferred_element_type=F32) + br_ref[...]
    lane = lax.broadcasted_iota(I32, logits.shape, 1)
    lane_f = lane.astype(F32)
    g = jnp.where(lane < N_EXPERTS, logits, -jnp.inf)
    vals, idxs = [], []
    for _ in range(TOP_K):
        mx = jnp.max(g, axis=1, keepdims=True)
        idx = jnp.min(jnp.where(g == mx, lane_f, float(LANES)), axis=1, keepdims=True)
        vals.append(mx)
        idxs.append(idx.astype(I32))
        g = jnp.where(lane_f == idx, -jnp.inf, g)
    es = [jnp.exp(v - vals[0]) for v in vals]
    den = es[0] + es[1] + es[2] + es[3]
    idx_out = jnp.zeros(logits.shape, I32)
    gate_out = jnp.zeros(logits.shape, F32)
    for kk in range(TOP_K):
        idx_out = jnp.where(lane == kk, idxs[kk], idx_out)
        gate_out = jnp.where(lane == kk, es[kk] / den, gate_out)
    idx_ref[...] = idx_out
    gate_ref[...] = gate_out


def _mix_out(u, gv, sga, sgb, yb, x, gt1, sc2, sh2, g2, ws, bs_t, w_a, w_b, w_o, w_r, b_r, tm):
    m, d = x.shape
    nb, r, _ = gt1.shape
    tiles_per_mod = m // (nb * tm)
    cs = ws.shape[1]
    assert m % tm == 0 and tm % cs == 0
    gw = u.shape[1]
    row = lambda i: (i, 0)
    const2 = lambda i: (0, 0)
    mod_spec = pl.BlockSpec((1, r, d), lambda i: (i // tiles_per_mod, 0, 0))
    resident = dict(pipeline_mode=pl.Buffered(1))
    return pl.pallas_call(
        _mix_out_kernel,
        grid=(m // tm,),
        in_specs=[pl.BlockSpec((tm, gw), row), pl.BlockSpec((tm, gw), row),
                  pl.BlockSpec((tm, d), row), pl.BlockSpec((tm, d), row),
                  pl.BlockSpec((tm, ATTN_WIDTH), row), pl.BlockSpec((tm, d), row),
                  mod_spec, mod_spec, mod_spec,
                  pl.BlockSpec((1, d), const2),
                  pl.BlockSpec(ws.shape, lambda i: (0, 0, 0), **resident),
                  pl.BlockSpec(bs_t.shape, const2),
                  pl.BlockSpec(w_a.shape, const2, **resident),
                  pl.BlockSpec(w_b.shape, const2, **resident),
                  pl.BlockSpec(w_o.shape, const2, **resident),
                  pl.BlockSpec(w_r.shape, const2, **resident),
                  pl.BlockSpec((1, LANES), const2)],
        out_specs=[pl.BlockSpec((tm, d), row), pl.BlockSpec((tm, d), row),
                   pl.BlockSpec((tm, LANES), row), pl.BlockSpec((tm, LANES), row)],
        out_shape=[jax.ShapeDtypeStruct((m, d), F32), jax.ShapeDtypeStruct((m, d), F32),
                   jax.ShapeDtypeStruct((m, LANES), I32), jax.ShapeDtypeStruct((m, LANES), F32)],
        scratch_shapes=[pltpu.VMEM((tm, gw), BF16)],
        compiler_params=_cparams(("arbitrary",)),
        name="mix_out",
    )(u, gv, sga, sgb, yb, x, gt1, sc2, sh2, g2.reshape(1, d), ws, bs_t, w_a, w_b, w_o, w_r, b_r)


def _gather_kernel(src_ref, nused_ref, h_hbm, o_ref, buf, sem):
    rb = pl.program_id(0)
    tb = o_ref.shape[0]
    slot = rb % 2
    nused = nused_ref[0]

    def row_copy(blk, sl, i):
        tok = src_ref[blk * tb + i]
        return pltpu.make_async_copy(h_hbm.at[pl.ds(tok, 1), :], buf.at[sl, pl.ds(i, 1), :], sem.at[sl])

    def start_block(blk, sl):
        def body(i, carry):
            row_copy(blk, sl, i).start()
            return carry
        lax.fori_loop(0, tb, body, 0, unroll=8)

    @pl.when(rb == 0)
    def _():
        start_block(rb, slot)

    @pl.when(rb + 1 < nused)
    def _():
        start_block(rb + 1, 1 - slot)

    @pl.when(rb < nused)
    def _():
        def body(i, carry):
            row_copy(rb, slot, i).wait()
            return carry
        lax.fori_loop(0, tb, body, 0, unroll=8)
        o_ref[...] = buf[slot].astype(o_ref.dtype)

    @pl.when(rb >= nused)
    def _():
        o_ref[...] = jnp.zeros(o_ref.shape, o_ref.dtype)


def _gather_rows(h2, src_tok, n_used, n_blocks):
    d = h2.shape[1]
    return pl.pallas_call(
        _gather_kernel,
        grid_spec=pltpu.PrefetchScalarGridSpec(
            num_scalar_prefetch=2,
            grid=(n_blocks,),
            in_specs=[pl.BlockSpec(memory_space=pl.ANY)],
            out_specs=pl.BlockSpec((MOE_ROWS, d), lambda rb, src, nu: (rb, 0)),
            scratch_shapes=[pltpu.VMEM((2, MOE_ROWS, d), F32), pltpu.SemaphoreType.DMA((2,))]),
        out_shape=jax.ShapeDtypeStruct((n_blocks * MOE_ROWS, d), BF16),
        compiler_params=_cparams(("arbitrary",)),
        name="moe_gather",
    )(src_tok, n_used, h2)


def _expert_changed(be_ref, rb):
    prev = be_ref[jnp.maximum(rb - 1, 0)]
    return (rb == 0) | (be_ref[rb] != prev)


def _up_kernel(be_ref, nused_ref, x_ref, wg_ref, wl_ref, bg_ref, bl_ref, o_ref, wg_scr, wl_scr):
    rb = pl.program_id(1)

    @pl.when(_expert_changed(be_ref, rb))
    def _():
        wg_scr[...] = wg_ref[0].astype(BF16)
        wl_scr[...] = wl_ref[0].astype(BF16)

    @pl.when(rb < nused_ref[0])
    def _():
        x = x_ref[...]
        glu = jnp.dot(x, wg_scr[...], preferred_element_type=F32) + bg_ref[0]
        lin = jnp.dot(x, wl_scr[...], preferred_element_type=F32) + bl_ref[0]
        glu = jnp.minimum(glu, SWIGLU_LIMIT)
        lin = jnp.clip(lin, -SWIGLU_LIMIT, SWIGLU_LIMIT)
        o_ref[...] = (glu * jax.nn.sigmoid(SWIGLU_ALPHA * glu) * (lin + 1.0)).astype(o_ref.dtype)

    @pl.when(rb >= nused_ref[0])
    def _():
        o_ref[...] = jnp.zeros(o_ref.shape, o_ref.dtype)


def _expert_up(xs, w1, b1, block_e, n_used, hc):
    p, d = xs.shape
    de = w1.shape[2] // 2
    n_blocks = p // MOE_ROWS
    nh = de // hc
    b1r = b1.reshape(N_EXPERTS, 1, 2 * de)
    rows = lambda n, rb, be, nu: (jnp.minimum(rb, nu[0] - 1), 0)
    return pl.pallas_call(
        _up_kernel,
        grid_spec=pltpu.PrefetchScalarGridSpec(
            num_scalar_prefetch=2,
            grid=(nh, n_blocks),
            in_specs=[pl.BlockSpec((MOE_ROWS, d), rows),
                      pl.BlockSpec((1, d, hc), lambda n, rb, be, nu: (be[rb], 0, n)),
                      pl.BlockSpec((1, d, hc), lambda n, rb, be, nu: (be[rb], 0, nh + n)),
                      pl.BlockSpec((1, 1, hc), lambda n, rb, be, nu: (be[rb], 0, n)),
                      pl.BlockSpec((1, 1, hc), lambda n, rb, be, nu: (be[rb], 0, nh + n))],
            out_specs=pl.BlockSpec((MOE_ROWS, hc), lambda n, rb, be, nu: (rb, n)),
            scratch_shapes=[pltpu.VMEM((d, hc), BF16), pltpu.VMEM((d, hc), BF16)]),
        out_shape=jax.ShapeDtypeStruct((p, de), BF16),
        compiler_params=_cparams(("arbitrary", "arbitrary")),
        name="expert_up",
    )(block_e, n_used, xs, w1, w1, b1r, b1r)


def _down_kernel(be_ref, nused_ref, a_ref, w_ref, b_ref, o_ref, w_scr):
    rb = pl.program_id(1)

    @pl.when(_expert_changed(be_ref, rb))
    def _():
        w_scr[...] = w_ref[0].astype(BF16)

    @pl.when(rb < nused_ref[0])
    def _():
        o_ref[...] = jnp.dot(a_ref[...], w_scr[...], preferred_element_type=F32) + b_ref[0]

    @pl.when(rb >= nused_ref[0])
    def _():
        o_ref[...] = jnp.zeros(o_ref.shape, o_ref.dtype)


def _expert_down(act, w2, b2, block_e, n_used, oc):
    p, de = act.shape
    d = w2.shape[2]
    n_blocks = p // MOE_ROWS
    b2r = b2.reshape(N_EXPERTS, 1, d)
    return pl.pallas_call(
        _down_kernel,
        grid_spec=pltpu.PrefetchScalarGridSpec(
            num_scalar_prefetch=2,
            grid=(d // oc, n_blocks),
            in_specs=[pl.BlockSpec((MOE_ROWS, de), lambda n, rb, be, nu: (jnp.minimum(rb, nu[0] - 1), 0)),
                      pl.BlockSpec((1, de, oc), lambda n, rb, be, nu: (be[rb], 0, n)),
                      pl.BlockSpec((1, 1, oc), lambda n, rb, be, nu: (be[rb], 0, n))],
            out_specs=pl.BlockSpec((MOE_ROWS, oc), lambda n, rb, be, nu: (rb, n)),
            scratch_shapes=[pltpu.VMEM((de, oc), BF16)]),
        out_shape=jax.ShapeDtypeStruct((p, d), F32),
        compiler_params=_cparams(("arbitrary", "arbitrary")),
        name="expert_down",
    )(block_e, n_used, act, w2, b2r)


def _combine_kernel(dest_ref, y_hbm, x1_ref, gate_ref, gt2_ref, o_ref, buf, sem, *, tok0):
    i = pl.program_id(0)
    n = pl.num_programs(0)
    tm = o_ref.shape[0]
    slot = i % 2

    def row_copy(tile, sl, t, kk):
        src = dest_ref[(tok0 + tile * tm + t) * TOP_K + kk]
        return pltpu.make_async_copy(y_hbm.at[pl.ds(src, 1), :], buf.at[sl, kk, pl.ds(t, 1), :], sem.at[sl])

    def for_rows(tile, sl, wait):
        def body(t, carry):
            for kk in range(TOP_K):
                cp = row_copy(tile, sl, t, kk)
                cp.wait() if wait else cp.start()
            return carry
        lax.fori_loop(0, tm, body, 0, unroll=2)

    @pl.when(i == 0)
    def _():
        for_rows(i, slot, False)

    @pl.when(i + 1 < n)
    def _():
        for_rows(i + 1, 1 - slot, False)

    for_rows(i, slot, True)
    gates = gate_ref[...]
    ff = gates[:, 0:1] * buf[slot, 0]
    for kk in range(1, TOP_K):
        ff = ff + gates[:, kk:kk + 1] * buf[slot, kk]
    o_ref[...] = x1_ref[...] + gt2_ref[0] * ff


def _combine(y_sorted, dest, x1, gates, gt2, tok0, tm):
    m, d = x1.shape
    nb, r, _ = gt2.shape
    tiles_per_mod = m // (nb * tm)
    kern = functools.partial(_combine_kernel, tok0=tok0)
    return pl.pallas_call(
        kern,
        grid_spec=pltpu.PrefetchScalarGridSpec(
            num_scalar_prefetch=1,
            grid=(m // tm,),
            in_specs=[pl.BlockSpec(memory_space=pl.ANY),
                      pl.BlockSpec((tm, d), lambda i, de: (i, 0)),
                      pl.BlockSpec((tm, LANES), lambda i, de: (i, 0)),
                      pl.BlockSpec((1, r, d), lambda i, de: (i // tiles_per_mod, 0, 0))],
            out_specs=pl.BlockSpec((tm, d), lambda i, de: (i, 0)),
            scratch_shapes=[pltpu.VMEM((2, TOP_K, tm, d), F32), pltpu.SemaphoreType.DMA((2,))]),
        out_shape=jax.ShapeDtypeStruct((m, d), F32),
        compiler_params=_cparams(("arbitrary",)),
        name="moe_combine",
    )(dest, y_sorted, x1, gates, gt2)


def _rope_tables(pos):
    half = HEAD_DIM // 2
    inv_freq = ROPE_THETA ** (-jnp.arange(half, dtype=F32) / half)
    ang = pos.astype(F32)[:, None] * inv_freq[None, :]
    cos = jnp.cos(ang)
    sin = jnp.sin(ang)
    return jnp.concatenate([cos, cos], axis=1), jnp.concatenate([-sin, sin], axis=1)


def _routing_tables(top_idx, n_blocks):
    e_flat = top_idx.reshape(-1)
    a = e_flat.shape[0]
    onehot = (e_flat[:, None] == jnp.arange(N_EXPERTS, dtype=I32)[None, :]).astype(I32)
    csum = jnp.cumsum(onehot, axis=0)
    counts = csum[-1]
    rank = jnp.sum(csum * onehot, axis=1) - 1
    padded = (counts + MOE_ROWS - 1) // MOE_ROWS * MOE_ROWS
    pad_end = jnp.cumsum(padded)
    pad_start = pad_end - padded
    dest = pad_start[e_flat] + rank
    src_tok = jnp.zeros((n_blocks * MOE_ROWS,), I32).at[dest].set(jnp.arange(a, dtype=I32) // TOP_K)
    block_e = jnp.minimum(
        jnp.searchsorted(pad_end, jnp.arange(n_blocks, dtype=I32) * MOE_ROWS, side='right'), N_EXPERTS - 1).astype(I32)
    n_used = (pad_end[-1:] // MOE_ROWS).astype(I32)
    return dest.astype(I32), src_tok, block_e, n_used


def kernel(x_prompt, x_sample, cache_k, cache_v, page_table, c_prompt, c_sample, w_ada, b_ada, g_norm1, g_norm2, w_in, b_in, g_q, g_k, gmlp_ln_g, gmlp_ln_b, w_spatial, b_spatial, w_branch_a, w_branch_b, w_out, w_router, b_router, w_expert_in, b_expert_in, w_expert_out, b_expert_out):
    depth = w_ada.shape[0]
    assert depth == 1
    bp, sp, d = x_prompt.shape
    bs_, ss, _ = x_sample.shape
    n_pages = page_table.shape[1]
    page = cache_k.shape[2]
    past_len = n_pages * page
    mp, ms = bp * sp, bs_ * ss
    tm_p, tm_s = 512, ms

    c_all = jnp.concatenate([c_prompt, c_sample], axis=0)
    pad = (-c_all.shape[0]) % 16
    c_all = jnp.pad(c_all, ((0, pad), (0, 0)))
    mod = _adaln(c_all, w_ada[0], b_ada[0])
    mod_p = [mod[:bp, i * d:(i + 1) * d].reshape(bp, 1, d) for i in range(6)]
    mod_s = [jnp.repeat(mod[bp:bp + bs_, i * d:(i + 1) * d], ss, axis=0).reshape(1, ms, d) for i in range(6)]

    w_in_b = w_in[0].astype(BF16)
    cos_p, sin_p = _rope_tables(jnp.arange(sp, dtype=I32))
    cos_s, sin_s = _rope_tables(past_len + jnp.arange(ss, dtype=I32))
    cos_s, sin_s = jnp.tile(cos_s, (bs_, 1)), jnp.tile(sin_s, (bs_, 1))

    xp = x_prompt.reshape(mp, d)
    xs = x_sample.reshape(ms, d)
    proj = functools.partial(_in_proj, g1=g_norm1[0], w_in=w_in_b, b_in=b_in[0], g_q=g_q[0], g_k=g_k[0],
                             ln_g=gmlp_ln_g[0], ln_b=gmlp_ln_b[0])
    qp, kp, vp, up, gvp, sgap, sgbp = proj(xp, mod_p[1], mod_p[0], cos=cos_p, sin=sin_p, tm=tm_p)
    qs, ks, vs, us, gvs, sgas, sgbs = proj(xs, mod_s[1], mod_s[0], cos=cos_s, sin=sin_s, tm=tm_s)

    yb_p = _moba_prompt(qp.reshape(bp, sp, ATTN_WIDTH), kp.reshape(bp, sp, ATTN_WIDTH),
                        vp.reshape(bp, sp, ATTN_WIDTH)).reshape(mp, ATTN_WIDTH)
    n_phys = cache_k.shape[1]
    yb_s = _moba_sample(qs.reshape(bs_, ss, ATTN_WIDTH), ks.reshape(bs_, ss, ATTN_WIDTH),
                        vs.reshape(bs_, ss, ATTN_WIDTH), cache_k[0].reshape(n_phys, page, ATTN_WIDTH),
                        cache_v[0].reshape(n_phys, page, ATTN_WIDTH), page_table).reshape(ms, ATTN_WIDTH)

    ws_p = w_spatial[0]
    bs_p = jnp.transpose(b_spatial[0])
    ws_s = jnp.einsum('ab,gts->gatbs', jnp.eye(bs_, dtype=F32), w_spatial[0][:, :ss, :ss]).reshape(
        GMLP_GROUPS, ms, ms)
    bs_s = jnp.tile(bs_p[:ss], (bs_, 1))
    w_a_b, w_b_b, w_o_b = w_branch_a[0].astype(BF16), w_branch_b[0].astype(BF16), w_out[0].astype(BF16)
    w_r_pad = jnp.pad(w_router[0], ((0, 0), (0, LANES - N_EXPERTS)))
    b_r_pad = jnp.pad(b_router[0], (0, LANES - N_EXPERTS)).reshape(1, LANES)
    mix = functools.partial(_mix_out, g2=g_norm2[0], w_a=w_a_b, w_b=w_b_b, w_o=w_o_b, w_r=w_r_pad, b_r=b_r_pad)
    x1p, h2p, idxp, gatep = mix(up, gvp, sgap, sgbp, yb_p, xp, mod_p[2], mod_p[4], mod_p[3],
                                ws=ws_p, bs_t=bs_p, tm=256)
    x1s, h2s, idxs, gates = mix(us, gvs, sgas, sgbs, yb_s, xs, mod_s[2], mod_s[4], mod_s[3],
                                ws=ws_s, bs_t=bs_s, tm=ms)

    h2 = jnp.concatenate([h2p, h2s], axis=0)
    top_idx = jnp.concatenate([idxp[:, :TOP_K], idxs[:, :TOP_K]], axis=0)
    n_tok = mp + ms
    n_assign = n_tok * TOP_K
    n_blocks = (n_assign + N_EXPERTS * (MOE_ROWS - 1) + MOE_ROWS - 1) // MOE_ROWS
    dest, src_tok, block_e, n_used = _routing_tables(top_idx, n_blocks)
    x_sorted = _gather_rows(h2, src_tok, n_used, n_blocks)
    act = _expert_up(x_sorted, w_expert_in[0], b_expert_in[0], block_e, n_used, hc=512)
    y_sorted = _expert_down(act, w_expert_out[0], b_expert_out[0], block_e, n_used, oc=512)
    yp = _combine(y_sorted, dest, x1p, gatep, mod_p[5], tok0=0, tm=256)
    ys = _combine(y_sorted, dest, x1s, gates, mod_s[5], tok0=mp, tm=ms)

    shp = (depth, bp, sp, N_HEADS, HEAD_DIM)
    shs = (depth, bs_, ss, N_HEADS, HEAD_DIM)
    return (yp.reshape(bp, sp, d), ys.reshape(bs_, ss, d), kp.reshape(shp), vp.reshape(shp),
            ks.reshape(shs), vs.reshape(shs), gvs.reshape(depth, bs_, ss, gvs.shape[1]))
```

```python
import functools
import math

import jax
import jax.numpy as jnp
from jax import lax
from jax.experimental import pallas as pl
from jax.experimental.pallas import tpu as pltpu

F32 = jnp.float32
BF16 = jnp.bfloat16
I32 = jnp.int32
HIGHEST = lax.Precision.HIGHEST

N_HEADS = 8
HEAD_DIM = 128
ATTN_WIDTH = N_HEADS * HEAD_DIM
MOBA_BLOCK = 256
MOBA_TOPK = 3
ROPE_THETA = 10000.0
GMLP_GROUPS = 8
GMLP_GROUP_DIM = 128
CHUNK = 128
N_EXPERTS = 32
TOP_K = 4
SWIGLU_ALPHA = 1.702
SWIGLU_LIMIT = 7.0
EPS = 1e-6
NEG_BIG = -1e30

LANES = 128
MOE_ROWS = 256
PAGES_PER_STEP = 8
VMEM_LIMIT = 56 * 1024 * 1024


def _cparams(sem):
    return pltpu.CompilerParams(dimension_semantics=sem, vmem_limit_bytes=VMEM_LIMIT)


def _ada_kernel(c_ref, w_ref, b_ref, o_ref):
    c = c_ref[...]
    s = (c * jax.nn.sigmoid(c)).astype(BF16)
    o_ref[...] = jnp.dot(s, w_ref[...].astype(BF16), preferred_element_type=F32) + b_ref[...]


def _adaln(c, w_ada, b_ada):
    m, d = c.shape
    n = w_ada.shape[1]
    tn = 1024
    return pl.pallas_call(
        _ada_kernel,
        grid=(n // tn,),
        in_specs=[pl.BlockSpec((m, d), lambda j: (0, 0)),
                  pl.BlockSpec((d, tn), lambda j: (0, j)),
                  pl.BlockSpec((1, tn), lambda j: (0, j))],
        out_specs=pl.BlockSpec((m, tn), lambda j: (0, j)),
        out_shape=jax.ShapeDtypeStruct((m, n), F32),
        compiler_params=_cparams(("arbitrary",)),
        name="adaln",
    )(c, w_ada, b_ada.reshape(1, n))


def _gelu(x):
    return 0.5 * x * (1.0 + lax.erf(x * (1.0 / math.sqrt(2.0))))


def _in_kernel(x_ref, sc_ref, sh_ref, g1_ref, w_ref, b_ref, gq_ref, gk_ref, cos_ref, sin_ref, lng_ref, lnb_ref,
               q_ref, k_ref, v_ref, u_ref, gv_ref, sga_ref, sgb_ref, h_scr):
    j = pl.program_id(1)

    @pl.when(j == 0)
    def _():
        x = x_ref[...]
        r = lax.rsqrt(jnp.mean(x * x, axis=-1, keepdims=True) + EPS)
        h = (x * r * g1_ref[...]) * (1.0 + sc_ref[0]) + sh_ref[0]
        h_scr[...] = h.astype(BF16)

    z = jnp.dot(h_scr[...], w_ref[...], preferred_element_type=F32) + b_ref[...]

    def qk_post(g_ref, o_ref):
        cos = cos_ref[...]
        sin = sin_ref[...]
        g = g_ref[...]
        for hh in range(N_HEADS):
            zh = z[:, hh * HEAD_DIM:(hh + 1) * HEAD_DIM]
            r = lax.rsqrt(jnp.mean(zh * zh, axis=-1, keepdims=True) + EPS)
            y = zh * r * g
            o_ref[:, hh * HEAD_DIM:(hh + 1) * HEAD_DIM] = y * cos + pltpu.roll(y, HEAD_DIM // 2, 1) * sin

    @pl.when(j == 0)
    def _():
        qk_post(gq_ref, q_ref)

    @pl.when(j == 1)
    def _():
        qk_post(gk_ref, k_ref)

    @pl.when(j == 2)
    def _():
        v_ref[...] = z

    @pl.when(j == 3)
    def _():
        u_ref[...] = _gelu(z).astype(u_ref.dtype)

    @pl.when(j == 4)
    def _():
        a = _gelu(z)
        mu = jnp.mean(a, axis=-1, keepdims=True)
        d = a - mu
        var = jnp.mean(d * d, axis=-1, keepdims=True)
        gv_ref[...] = d * lax.rsqrt(var + EPS) * lng_ref[...] + lnb_ref[...]

    @pl.when((j == 5) | (j == 6))
    def _():
        sga_ref[...] = jax.nn.sigmoid(z).astype(sga_ref.dtype)

    @pl.when(j >= 7)
    def _():
        sgb_ref[...] = jax.nn.sigmoid(z).astype(sgb_ref.dtype)


def _in_proj(x, sc, sh, g1, w_in, b_in, g_q, g_k, cos, sin, ln_g, ln_b, tm):
    m, d = x.shape
    n = w_in.shape[1]
    tn = 1024
    assert n == 9 * tn and m % tm == 0
    nb, r, _ = sc.shape
    tiles_per_mod = m // (nb * tm)
    tiles_per_tab = cos.shape[0] // tm
    row = lambda i, j: (i, 0)
    mod_spec = pl.BlockSpec((1, r, d), lambda i, j: (i // tiles_per_mod, 0, 0))
    vec_d = pl.BlockSpec((1, d), lambda i, j: (0, 0))
    vec_h = pl.BlockSpec((1, HEAD_DIM), lambda i, j: (0, 0))
    vec_n = pl.BlockSpec((1, tn), lambda i, j: (0, 0))
    tab = pl.BlockSpec((tm, HEAD_DIM), lambda i, j: (i % tiles_per_tab, 0))
    outs = pl.pallas_call(
        _in_kernel,
        grid=(m // tm, n // tn),
        in_specs=[pl.BlockSpec((tm, d), row), mod_spec, mod_spec, vec_d,
                  pl.BlockSpec((d, tn), lambda i, j: (0, j)),
                  pl.BlockSpec((1, tn), lambda i, j: (0, j)),
                  vec_h, vec_h, tab, tab, vec_n, vec_n],
        out_specs=[pl.BlockSpec((tm, tn), row), pl.BlockSpec((tm, tn), row), pl.BlockSpec((tm, tn), row),
                   pl.BlockSpec((tm, tn), row), pl.BlockSpec((tm, tn), row),
                   pl.BlockSpec((tm, tn), lambda i, j: (i, jnp.clip(j - 5, 0, 1))),
                   pl.BlockSpec((tm, tn), lambda i, j: (i, jnp.clip(j - 7, 0, 1)))],
        out_shape=[jax.ShapeDtypeStruct((m, tn), F32), jax.ShapeDtypeStruct((m, tn), F32),
                   jax.ShapeDtypeStruct((m, tn), F32), jax.ShapeDtypeStruct((m, tn), BF16),
                   jax.ShapeDtypeStruct((m, tn), F32), jax.ShapeDtypeStruct((m, 2 * tn), BF16),
                   jax.ShapeDtypeStruct((m, 2 * tn), BF16)],
        scratch_shapes=[pltpu.VMEM((tm, d), BF16)],
        compiler_params=_cparams(("arbitrary", "arbitrary")),
        name="in_proj",
    )(x, sc, sh, g1.reshape(1, d), w_in, b_in.reshape(1, n), g_q.reshape(1, HEAD_DIM), g_k.reshape(1, HEAD_DIM),
      cos, sin, ln_g.reshape(1, tn), ln_b.reshape(1, tn))
    return outs


def _moba_prompt_kernel(q_ref, k_ref, v_ref, o_ref):
    s_len = q_ref.shape[1]
    nb = s_len // MOBA_BLOCK
    blk = MOBA_BLOCK
    q = q_ref[0]
    k = k_ref[0]
    means = jnp.concatenate(
        [jnp.mean(k[n * blk:(n + 1) * blk], axis=0, keepdims=True) for n in range(nb)], axis=0)
    gate = lax.dot_general(means, q, (((1,), (1,)), ((), ())), precision=HIGHEST,
                           preferred_element_type=F32)
    n_iota = lax.broadcasted_iota(I32, (nb, s_len), 0)
    own = lax.broadcasted_iota(I32, (nb, s_len), 1) // blk
    cnt = jnp.zeros((nb, s_len), I32)
    for m in range(nb):
        gm = gate[m:m + 1, :]
        beats = (m < own) & ((gm > gate) | ((gm == gate) & (m < n_iota)))
        cnt = cnt + beats.astype(I32)
    sel = ((n_iota < own) & (cnt < MOBA_TOPK)).astype(F32)
    sel = jnp.concatenate([sel, jnp.zeros((LANES - nb, s_len), F32)], axis=0)

    qb = q.astype(BF16)
    kb = k.astype(BF16)
    vb = v_ref[0].astype(BF16)
    scale = HEAD_DIM ** -0.5
    row = lax.broadcasted_iota(I32, (blk, blk), 0)
    col = lax.broadcasted_iota(I32, (blk, blk), 1)
    for i in range(nb):
        w = (i + 1) * blk
        s = lax.dot_general(qb[i * blk:(i + 1) * blk], kb[:w], (((1,), (1,)), ((), ())),
                            preferred_element_type=F32) * scale
        sel_t = sel[:, i * blk:(i + 1) * blk].T
        parts = [jnp.broadcast_to(sel_t[:, jj:jj + 1], (blk, blk)) for jj in range(i)]
        parts.append(jnp.where(col <= row, 1.0, 0.0))
        allowed = jnp.concatenate(parts, axis=1) if i else parts[0]
        s = jnp.where(allowed > 0.0, s, NEG_BIG)
        mx = jnp.max(s, axis=1, keepdims=True)
        p = jnp.exp(s - mx)
        l = jnp.sum(p, axis=1, keepdims=True)
        o = jnp.dot(p.astype(BF16), vb[:w], preferred_element_type=F32) / l
        o_ref[0, i * blk:(i + 1) * blk, :] = o.astype(o_ref.dtype)


def _moba_prompt(q, k, v):
    b, s, _ = q.shape
    assert s % MOBA_BLOCK == 0
    spec = pl.BlockSpec((1, s, HEAD_DIM), lambda bi, hi: (bi, 0, hi))
    return pl.pallas_call(
        _moba_prompt_kernel,
        grid=(b, N_HEADS),
        in_specs=[spec, spec, spec],
        out_specs=spec,
        out_shape=jax.ShapeDtypeStruct(q.shape, BF16),
        compiler_params=_cparams(("arbitrary", "arbitrary")),
        name="moba_prompt",
    )(q, k, v)


def _moba_sample_kernel(pt_ref, q_ref, kn_ref, vn_ref, kc_hbm, vc_hbm, o_ref,
                        buf, sem, qbd_scr, sc_scr, p_scr, means_scr, bmax_scr, ebig_scr, selx_scr,
                        pown_scr, l_scr, acc_scr, *, n_pages, n_req):
    pg = PAGES_PER_STEP
    ns = n_pages // pg
    nblk = n_pages // 2
    n_tok = kn_ref.shape[1]
    rows = N_HEADS * n_tok
    page_rows = buf.shape[2]
    r = pl.program_id(0)
    c = pl.program_id(1)
    step = r * (2 * ns) + c
    total = n_req * 2 * ns
    slot = step % 2
    scale = HEAD_DIM ** -0.5

    def page_copy(st, sl, p):
        r_ = st // (2 * ns)
        c_ = st % (2 * ns)
        page = pt_ref[r_ * n_pages + (c_ % ns) * pg + p]
        src = pl.ds(pl.multiple_of(page * page_rows, page_rows), page_rows)
        return c_ < ns, (pltpu.make_async_copy(kc_hbm.at[src, :], buf.at[sl, p], sem.at[sl, p]),
                         pltpu.make_async_copy(vc_hbm.at[src, :], buf.at[sl, p], sem.at[sl, p]))

    def fetch(st, sl, wait):
        for p in range(pg):
            is_k, (ck, cv) = page_copy(st, sl, p)

            @pl.when(is_k)
            def _():
                ck.wait() if wait else ck.start()

            @pl.when(jnp.logical_not(is_k))
            def _():
                cv.wait() if wait else cv.start()

    def load_page(p):
        return jnp.concatenate([buf[slot, p, pl.ds(hh, LANES, stride=N_HEADS), :] for hh in range(N_HEADS)],
                               axis=1)

    @pl.when(step == 0)
    def _():
        fetch(step, slot, False)
        rb = lax.broadcasted_iota(I32, ebig_scr.shape, 0)
        cb = lax.broadcasted_iota(I32, ebig_scr.shape, 1) // LANES
        ebig_scr[...] = jnp.where(rb == cb, 1.0, 0.0).astype(BF16)

    @pl.when(step + 1 < total)
    def _():
        fetch(step + 1, 1 - slot, False)

    fetch(step, slot, True)

    @pl.when(c == 0)
    def _():
        q8 = q_ref[0]
        qrep = jnp.concatenate([q8] * N_HEADS, axis=0)
        rh = lax.broadcasted_iota(I32, qrep.shape, 0) // n_tok
        ch = lax.broadcasted_iota(I32, qrep.shape, 1) // HEAD_DIM
        qbd_scr[...] = jnp.where(rh == ch, qrep, 0.0)
        bmax_scr[...] = jnp.full(bmax_scr.shape, NEG_BIG, F32)

    @pl.when(c < ns)
    def _():
        qbd = qbd_scr[...].astype(BF16)
        lane = lax.broadcasted_iota(I32, bmax_scr.shape, 1)
        prev_sum = prev_s = None
        for p in range(pg):
            kp = load_page(p)
            s = lax.dot_general(qbd, kp.astype(BF16), (((1,), (1,)), ((), ())), preferred_element_type=F32)
            b_idx = (c * pg + p) // 2
            half = p % 2
            sc_scr[b_idx, :, half * LANES:(half + 1) * LANES] = s
            psum = jnp.sum(kp, axis=0, keepdims=True)
            if half == 0:
                prev_sum, prev_s = psum, s
            else:
                means_scr[pl.ds(b_idx, 1), :] = (prev_sum + psum) * (1.0 / MOBA_BLOCK)
                bm = jnp.max(jnp.maximum(prev_s, s), axis=1, keepdims=True)
                bmax_scr[...] = jnp.where(lane == b_idx, bm, bmax_scr[...])

    @pl.when(c == ns - 1)
    def _():
        qbd = qbd_scr[...]
        gate = lax.dot_general(qbd, means_scr[...], (((1,), (1,)), ((), ())), precision=HIGHEST,
                               preferred_element_type=F32)
        lane = lax.broadcasted_iota(I32, gate.shape, 1).astype(F32)
        g = gate
        sel = jnp.zeros(gate.shape, F32)
        for _ in range(MOBA_TOPK):
            mx = jnp.max(g, axis=1, keepdims=True)
            idx = jnp.min(jnp.where(g == mx, lane, float(nblk)), axis=1, keepdims=True)
            pick = lane == idx
            sel = jnp.where(pick, 1.0, sel)
            g = jnp.where(pick, -jnp.inf, g)
        selx_scr[...] = jnp.dot(sel.astype(BF16), ebig_scr[...], preferred_element_type=F32)

        kn = jnp.concatenate([kn_ref[0], jnp.zeros((LANES - n_tok, kn_ref.shape[2]), F32)], axis=0)
        s_own = lax.dot_general(qbd.astype(BF16), kn.astype(BF16), (((1,), (1,)), ((), ())),
                                preferred_element_type=F32) * scale
        t_q = lax.broadcasted_iota(I32, s_own.shape, 0) % n_tok
        t_k = lax.broadcasted_iota(I32, s_own.shape, 1)
        s_own = jnp.where(t_k <= t_q, s_own, NEG_BIG)

        m_sel = jnp.max(jnp.where(sel > 0.0, bmax_scr[:, :nblk] * scale, NEG_BIG), axis=1, keepdims=True)
        m = jnp.maximum(m_sel, jnp.max(s_own, axis=1, keepdims=True))
        m_b = jnp.broadcast_to(m, (rows, LANES))

        def exp_body(b_idx, lacc):
            off = pl.multiple_of(b_idx * LANES, LANES)
            msk = selx_scr[:, pl.ds(off, LANES)] > 0.0
            s = sc_scr[b_idx]
            p0 = jnp.where(msk, jnp.exp(s[:, :LANES] * scale - m_b), 0.0)
            p1 = jnp.where(msk, jnp.exp(s[:, LANES:] * scale - m_b), 0.0)
            p_scr[b_idx, :, :LANES] = p0.astype(BF16)
            p_scr[b_idx, :, LANES:] = p1.astype(BF16)
            return lacc + p0 + p1

        p_own = jnp.exp(s_own - m_b)
        lacc = lax.fori_loop(0, nblk, exp_body, p_own, unroll=4)
        pown_scr[...] = p_own.astype(BF16)
        l_scr[...] = jnp.broadcast_to(jnp.sum(lacc, axis=1, keepdims=True), l_scr.shape)

    @pl.when(c == ns)
    def _():
        vn = jnp.concatenate([vn_ref[0], jnp.zeros((LANES - n_tok, vn_ref.shape[2]), F32)], axis=0)
        acc_scr[...] = jnp.dot(pown_scr[...], vn.astype(BF16), preferred_element_type=F32)

    @pl.when(c >= ns)
    def _():
        acc = acc_scr[...]
        for p in range(pg):
            vp = load_page(p).astype(BF16)
            b_idx = ((c - ns) * pg + p) // 2
            half = p % 2
            acc = acc + jnp.dot(p_scr[b_idx, :, half * LANES:(half + 1) * LANES], vp, preferred_element_type=F32)
        acc_scr[...] = acc

    @pl.when(c == 2 * ns - 1)
    def _():
        o = acc_scr[...] / l_scr[:, 0:1]
        for hh in range(N_HEADS):
            o_ref[0, :, hh * HEAD_DIM:(hh + 1) * HEAD_DIM] = o[hh * n_tok:(hh + 1) * n_tok,
                                                               hh * HEAD_DIM:(hh + 1) * HEAD_DIM].astype(o_ref.dtype)


def _moba_sample(q, k_new, v_new, cache_k, cache_v, page_table, page):
    n_req, n_tok, width = q.shape
    n_pages = page_table.shape[1]
    assert page == LANES and 2 * page == MOBA_BLOCK and n_pages % (2 * PAGES_PER_STEP) == 0
    assert n_tok <= LANES and (N_HEADS * n_tok) % 8 == 0 and n_pages // 2 <= LANES
    ns = n_pages // PAGES_PER_STEP
    nblk = n_pages // 2
    rows = N_HEADS * n_tok
    tok_spec = pl.BlockSpec((1, n_tok, width), lambda r, c, pt: (r, 0, 0))
    kern = functools.partial(_moba_sample_kernel, n_pages=n_pages, n_req=n_req)
    return pl.pallas_call(
        kern,
        grid_spec=pltpu.PrefetchScalarGridSpec(
            num_scalar_prefetch=1,
            grid=(n_req, 2 * ns),
            in_specs=[tok_spec, tok_spec, tok_spec,
                      pl.BlockSpec(memory_space=pl.ANY), pl.BlockSpec(memory_space=pl.ANY)],
            out_specs=tok_spec,
            scratch_shapes=[pltpu.VMEM((2, PAGES_PER_STEP, page * N_HEADS, HEAD_DIM), F32),
                            pltpu.SemaphoreType.DMA((2, PAGES_PER_STEP)),
                            pltpu.VMEM((rows, width), F32),
                            pltpu.VMEM((nblk, rows, MOBA_BLOCK), F32),
                            pltpu.VMEM((nblk, rows, MOBA_BLOCK), BF16),
                            pltpu.VMEM((nblk, width), F32),
                            pltpu.VMEM((rows, LANES), F32),
                            pltpu.VMEM((nblk, nblk * LANES), BF16),
                            pltpu.VMEM((rows, nblk * LANES), F32),
                            pltpu.VMEM((rows, LANES), BF16),
                            pltpu.VMEM((rows, LANES), F32),
                            pltpu.VMEM((rows, width), F32)]),
        out_shape=jax.ShapeDtypeStruct(q.shape, BF16),
        compiler_params=_cparams(("arbitrary", "arbitrary")),
        name="moba_sample",
    )(page_table.reshape(-1), q, k_new, v_new, cache_k, cache_v)


def _mix_out_kernel(u_ref, gv_ref, sga_ref, sgb_ref, yb_ref, x_ref, gt1_ref, sc2_ref, sh2_ref, g2_ref,
                    ws_ref, bs_ref, wa_ref, wb_ref, wo_ref, wr_ref, br_ref,
                    x1_ref, h2_ref, idx_ref, gate_ref, ya_scr):
    tm = u_ref.shape[0]
    cs = ws_ref.shape[1]
    row = lax.broadcasted_iota(I32, (cs, cs), 0)
    col = lax.broadcasted_iota(I32, (cs, cs), 1)
    causal = col <= row
    bs = bs_ref[...]
    for g in range(GMLP_GROUPS):
        ws = jnp.where(causal, ws_ref[g], 0.0).astype(BF16)
        gs = slice(g * GMLP_GROUP_DIM, (g + 1) * GMLP_GROUP_DIM)
        for ci in range(tm // cs):
            rs = slice(ci * cs, (ci + 1) * cs)
            mixed = jnp.dot(ws, gv_ref[rs, gs].astype(BF16), preferred_element_type=F32) + bs[:, g:g + 1]
            ya_scr[rs, gs] = (u_ref[rs, gs].astype(F32) * mixed).astype(BF16)

    a = jnp.dot(ya_scr[...], wa_ref[...], preferred_element_type=F32)
    b = jnp.dot(yb_ref[...], wb_ref[...], preferred_element_type=F32)
    merged = sga_ref[...].astype(F32) * a + sgb_ref[...].astype(F32) * b
    mix = jnp.dot(merged.astype(BF16), wo_ref[...], preferred_element_type=F32)
    x1 = x_ref[...] + gt1_ref[0] * mix
    x1_ref[...] = x1
    r = lax.rsqrt(jnp.mean(x1 * x1, axis=-1, keepdims=True) + EPS)
    h2 = (x1 * r * g2_ref[...]) * (1.0 + sc2_ref[0]) + sh2_ref[0]
    h2_ref[...] = h2

    logits = jnp.dot(h2, wr_ref[...], precision=HIGHEST, preferred_element_type=F32) + br_ref[...]
    lane = lax.broadcasted_iota(I32, logits.shape, 1)
    lane_f = lane.astype(F32)
    g = jnp.where(lane < N_EXPERTS, logits, -jnp.inf)
    vals, idxs = [], []
    for _ in range(TOP_K):
        mx = jnp.max(g, axis=1, keepdims=True)
        idx = jnp.min(jnp.where(g == mx, lane_f, float(LANES)), axis=1, keepdims=True)
        vals.append(mx)
        idxs.append(idx.astype(I32))
        g = jnp.where(lane_f == idx, -jnp.inf, g)
    es = [jnp.exp(v - vals[0]) for v in vals]
    den = es[0] + es[1] + es[2] + es[3]
    idx_out = jnp.zeros(logits.shape, I32)
    gate_out = jnp.zeros(logits.shape, F32)
    for kk in range(TOP_K):
        idx_out = jnp.where(lane == kk, idxs[kk], idx_out)
        gate_out = jnp.where(lane == kk, es[kk] / den, gate_out)
    idx_ref[...] = idx_out
    gate_ref[...] = gate_out


def _mix_out(u, gv, sga, sgb, yb, x, gt1, sc2, sh2, g2, ws, bs_t, w_a, w_b, w_o, w_r, b_r, tm):
    m, d = x.shape
    nb, r, _ = gt1.shape
    tiles_per_mod = m // (nb * tm)
    cs = ws.shape[1]
    assert m % tm == 0 and tm % cs == 0
    gw = u.shape[1]
    row = lambda i: (i, 0)
    const2 = lambda i: (0, 0)
    mod_spec = pl.BlockSpec((1, r, d), lambda i: (i // tiles_per_mod, 0, 0))
    resident = dict(pipeline_mode=pl.Buffered(1))
    return pl.pallas_call(
        _mix_out_kernel,
        grid=(m // tm,),
        in_specs=[pl.BlockSpec((tm, gw), row), pl.BlockSpec((tm, gw), row),
                  pl.BlockSpec((tm, d), row), pl.BlockSpec((tm, d), row),
                  pl.BlockSpec((tm, ATTN_WIDTH), row), pl.BlockSpec((tm, d), row),
                  mod_spec, mod_spec, mod_spec,
                  pl.BlockSpec((1, d), const2),
                  pl.BlockSpec(ws.shape, lambda i: (0, 0, 0), **resident),
                  pl.BlockSpec(bs_t.shape, const2),
                  pl.BlockSpec(w_a.shape, const2, **resident),
                  pl.BlockSpec(w_b.shape, const2, **resident),
                  pl.BlockSpec(w_o.shape, const2, **resident),
                  pl.BlockSpec(w_r.shape, const2, **resident),
                  pl.BlockSpec((1, LANES), const2)],
        out_specs=[pl.BlockSpec((tm, d), row), pl.BlockSpec((tm, d), row),
                   pl.BlockSpec((tm, LANES), row), pl.BlockSpec((tm, LANES), row)],
        out_shape=[jax.ShapeDtypeStruct((m, d), F32), jax.ShapeDtypeStruct((m, d), F32),
                   jax.ShapeDtypeStruct((m, LANES), I32), jax.ShapeDtypeStruct((m, LANES), F32)],
        scratch_shapes=[pltpu.VMEM((tm, gw), BF16)],
        compiler_params=_cparams(("arbitrary",)),
        name="mix_out",
    )(u, gv, sga, sgb, yb, x, gt1, sc2, sh2, g2.reshape(1, d), ws, bs_t, w_a, w_b, w_o, w_r, b_r)


def _gather_kernel(src_ref, nused_ref, h_hbm, o_ref, buf, sem):
    rb = pl.program_id(0)
    tb = o_ref.shape[0]
    slot = rb % 2
    nused = nused_ref[0]

    def row_copy(blk, sl, i):
        tok = src_ref[blk * tb + i]
        return pltpu.make_async_copy(h_hbm.at[pl.ds(tok, 1), :], buf.at[sl, pl.ds(i, 1), :], sem.at[sl])

    def start_block(blk, sl):
        def body(i, carry):
            row_copy(blk, sl, i).start()
            return carry
        lax.fori_loop(0, tb, body, 0, unroll=8)

    @pl.when(rb == 0)
    def _():
        start_block(rb, slot)

    @pl.when(rb + 1 < nused)
    def _():
        start_block(rb + 1, 1 - slot)

    @pl.when(rb < nused)
    def _():
        def body(i, carry):
            row_copy(rb, slot, i).wait()
            return carry
        lax.fori_loop(0, tb, body, 0, unroll=8)
        o_ref[...] = buf[slot].astype(o_ref.dtype)

    @pl.when(rb >= nused)
    def _():
        o_ref[...] = jnp.zeros(o_ref.shape, o_ref.dtype)


def _gather_rows(h2, src_tok, n_used, n_blocks):
    d = h2.shape[1]
    return pl.pallas_call(
        _gather_kernel,
        grid_spec=pltpu.PrefetchScalarGridSpec(
            num_scalar_prefetch=2,
            grid=(n_blocks,),
            in_specs=[pl.BlockSpec(memory_space=pl.ANY)],
            out_specs=pl.BlockSpec((MOE_ROWS, d), lambda rb, src, nu: (rb, 0)),
            scratch_shapes=[pltpu.VMEM((2, MOE_ROWS, d), F32), pltpu.SemaphoreType.DMA((2,))]),
        out_shape=jax.ShapeDtypeStruct((n_blocks * MOE_ROWS, d), BF16),
        compiler_params=_cparams(("arbitrary",)),
        name="moe_gather",
    )(src_tok, n_used, h2)


def _stream_weights(brun_ref, rune_ref, meta_ref, w_hbm, wbuf, sem, w_scrs, part_bases, width):
    n = pl.program_id(0)
    rb = pl.program_id(1)
    n_runs = meta_ref[1]
    run = brun_ref[rb]
    g = n * n_runs + run
    first = (rb == 0) | (brun_ref[jnp.maximum(rb - 1, 0)] != run)

    def copies(gg):
        e = rune_ref[gg % n_runs]
        nn = gg // n_runs
        sl = gg % 2
        return [pltpu.make_async_copy(
            w_hbm.at[e, :, pl.ds(pl.multiple_of(base + nn * width, width), width)], wbuf.at[sl, part], sem.at[sl, part])
            for part, base in enumerate(part_bases)]

    @pl.when(first)
    def _():
        @pl.when(g == 0)
        def _():
            for cp in copies(g):
                cp.start()

        @pl.when(g + 1 < pl.num_programs(0) * n_runs)
        def _():
            for cp in copies(g + 1):
                cp.start()

        for cp in copies(g):
            cp.wait()
        for part, scr in enumerate(w_scrs):
            scr[...] = wbuf[g % 2, part].astype(BF16)


def _up_kernel(be_ref, brun_ref, rune_ref, meta_ref, x_ref, bg_ref, bl_ref, w_hbm, o_ref,
               wbuf, sem, wg_scr, wl_scr, *, de):
    rb = pl.program_id(1)
    n_used = meta_ref[0]
    _stream_weights(brun_ref, rune_ref, meta_ref, w_hbm, wbuf, sem, (wg_scr, wl_scr), (0, de), wg_scr.shape[1])

    @pl.when(rb < n_used)
    def _():
        x = x_ref[...]
        glu = jnp.dot(x, wg_scr[...], preferred_element_type=F32) + bg_ref[0]
        lin = jnp.dot(x, wl_scr[...], preferred_element_type=F32) + bl_ref[0]
        glu = jnp.minimum(glu, SWIGLU_LIMIT)
        lin = jnp.clip(lin, -SWIGLU_LIMIT, SWIGLU_LIMIT)
        o_ref[...] = (glu * jax.nn.sigmoid(SWIGLU_ALPHA * glu) * (lin + 1.0)).astype(o_ref.dtype)

    @pl.when(rb >= n_used)
    def _():
        o_ref[...] = jnp.zeros(o_ref.shape, o_ref.dtype)


def _expert_up(xs, w1, b1, tables, hc):
    p, d = xs.shape
    de = w1.shape[2] // 2
    n_blocks = p // MOE_ROWS
    nh = de // hc
    b1r = b1.reshape(N_EXPERTS, 1, 2 * de)
    return pl.pallas_call(
        functools.partial(_up_kernel, de=de),
        grid_spec=pltpu.PrefetchScalarGridSpec(
            num_scalar_prefetch=4,
            grid=(nh, n_blocks),
            in_specs=[pl.BlockSpec((MOE_ROWS, d), lambda n, rb, be, br, re, mt: (jnp.minimum(rb, mt[0] - 1), 0)),
                      pl.BlockSpec((1, 1, hc), lambda n, rb, be, br, re, mt: (be[rb], 0, n)),
                      pl.BlockSpec((1, 1, hc), lambda n, rb, be, br, re, mt: (be[rb], 0, nh + n)),
                      pl.BlockSpec(memory_space=pl.ANY)],
            out_specs=pl.BlockSpec((MOE_ROWS, hc), lambda n, rb, be, br, re, mt: (rb, n)),
            scratch_shapes=[pltpu.VMEM((2, 2, d, hc), F32), pltpu.SemaphoreType.DMA((2, 2)),
                            pltpu.VMEM((d, hc), BF16), pltpu.VMEM((d, hc), BF16)]),
        out_shape=jax.ShapeDtypeStruct((p, de), BF16),
        compiler_params=_cparams(("arbitrary", "arbitrary")),
        name="expert_up",
    )(*tables, xs, b1r, b1r, w1)


def _down_kernel(be_ref, brun_ref, rune_ref, meta_ref, a_ref, b_ref, w_hbm, o_ref, wbuf, sem, w_scr):
    rb = pl.program_id(1)
    n_used = meta_ref[0]
    _stream_weights(brun_ref, rune_ref, meta_ref, w_hbm, wbuf, sem, (w_scr,), (0,), w_scr.shape[1])

    @pl.when(rb < n_used)
    def _():
        o_ref[...] = jnp.dot(a_ref[...], w_scr[...], preferred_element_type=F32) + b_ref[0]

    @pl.when(rb >= n_used)
    def _():
        o_ref[...] = jnp.zeros(o_ref.shape, o_ref.dtype)


def _expert_down(act, w2, b2, tables, oc):
    p, de = act.shape
    d = w2.shape[2]
    n_blocks = p // MOE_ROWS
    b2r = b2.reshape(N_EXPERTS, 1, d)
    return pl.pallas_call(
        _down_kernel,
        grid_spec=pltpu.PrefetchScalarGridSpec(
            num_scalar_prefetch=4,
            grid=(d // oc, n_blocks),
            in_specs=[pl.BlockSpec((MOE_ROWS, de), lambda n, rb, be, br, re, mt: (jnp.minimum(rb, mt[0] - 1), 0)),
                      pl.BlockSpec((1, 1, oc), lambda n, rb, be, br, re, mt: (be[rb], 0, n)),
                      pl.BlockSpec(memory_space=pl.ANY)],
            out_specs=pl.BlockSpec((MOE_ROWS, oc), lambda n, rb, be, br, re, mt: (rb, n)),
            scratch_shapes=[pltpu.VMEM((2, 1, de, oc), F32), pltpu.SemaphoreType.DMA((2, 1)),
                            pltpu.VMEM((de, oc), BF16)]),
        out_shape=jax.ShapeDtypeStruct((p, d), F32),
        compiler_params=_cparams(("arbitrary", "arbitrary")),
        name="expert_down",
    )(*tables, act, b2r, w2)


def _combine_kernel(dest_ref, y_hbm, x1_ref, gate_ref, gt2_ref, o_ref, buf, sem, *, tok0):
    i = pl.program_id(0)
    n = pl.num_programs(0)
    tm = o_ref.shape[0]
    slot = i % 2

    def row_copy(tile, sl, t, kk):
        src = dest_ref[(tok0 + tile * tm + t) * TOP_K + kk]
        return pltpu.make_async_copy(y_hbm.at[pl.ds(src, 1), :], buf.at[sl, kk, pl.ds(t, 1), :], sem.at[sl])

    def for_rows(tile, sl, wait):
        def body(t, carry):
            for kk in range(TOP_K):
                cp = row_copy(tile, sl, t, kk)
                cp.wait() if wait else cp.start()
            return carry
        lax.fori_loop(0, tm, body, 0, unroll=2)

    @pl.when(i == 0)
    def _():
        for_rows(i, slot, False)

    @pl.when(i + 1 < n)
    def _():
        for_rows(i + 1, 1 - slot, False)

    for_rows(i, slot, True)
    gates = gate_ref[...]
    ff = gates[:, 0:1] * buf[slot, 0]
    for kk in range(1, TOP_K):
        ff = ff + gates[:, kk:kk + 1] * buf[slot, kk]
    o_ref[...] = x1_ref[...] + gt2_ref[0] * ff


def _combine(y_sorted, dest, x1, gates, gt2, tok0, tm):
    m, d = x1.shape
    nb, r, _ = gt2.shape
    tiles_per_mod = m // (nb * tm)
    kern = functools.partial(_combine_kernel, tok0=tok0)
    return pl.pallas_call(
        kern,
        grid_spec=pltpu.PrefetchScalarGridSpec(
            num_scalar_prefetch=1,
            grid=(m // tm,),
            in_specs=[pl.BlockSpec(memory_space=pl.ANY),
                      pl.BlockSpec((tm, d), lambda i, de: (i, 0)),
                      pl.BlockSpec((tm, LANES), lambda i, de: (i, 0)),
                      pl.BlockSpec((1, r, d), lambda i, de: (i // tiles_per_mod, 0, 0))],
            out_specs=pl.BlockSpec((tm, d), lambda i, de: (i, 0)),
            scratch_shapes=[pltpu.VMEM((2, TOP_K, tm, d), F32), pltpu.SemaphoreType.DMA((2,))]),
        out_shape=jax.ShapeDtypeStruct((m, d), F32),
        compiler_params=_cparams(("arbitrary",)),
        name="moe_combine",
    )(dest, y_sorted, x1, gates, gt2)


def _rope_tables(pos):
    half = HEAD_DIM // 2
    inv_freq = ROPE_THETA ** (-jnp.arange(half, dtype=F32) / half)
    ang = pos.astype(F32)[:, None] * inv_freq[None, :]
    cos = jnp.cos(ang)
    sin = jnp.sin(ang)
    return jnp.concatenate([cos, cos], axis=1), jnp.concatenate([-sin, sin], axis=1)


def _routing_tables(top_idx, n_blocks):
    e_flat = top_idx.reshape(-1)
    a = e_flat.shape[0]
    ck = LANES
    assert a % ck == 0
    onehot = e_flat[:, None] == jnp.arange(N_EXPERTS, dtype=I32)[None, :]
    within = jnp.einsum('ts,csn->ctn', jnp.tril(jnp.ones((ck, ck), BF16)), onehot.astype(BF16).reshape(a // ck, ck, -1),
                        preferred_element_type=F32)
    tot = within[:, -1, :]
    before = jnp.dot(jnp.tril(jnp.ones((a // ck, a // ck), F32), -1), tot, precision=HIGHEST)
    csum = (within + before[:, None, :]).reshape(a, N_EXPERTS).astype(I32)
    counts = csum[-1]
    rank = jnp.sum(jnp.where(onehot, csum, 0), axis=1) - 1
    padded = (counts + MOE_ROWS - 1) // MOE_ROWS * MOE_ROWS
    pad_end = jnp.cumsum(padded)
    pad_start = pad_end - padded
    dest = pad_start[e_flat] + rank
    src_tok = jnp.zeros((n_blocks * MOE_ROWS,), I32).at[dest].set(jnp.arange(a, dtype=I32) // TOP_K)
    n_used = pad_end[-1] // MOE_ROWS
    blk = jnp.arange(n_blocks, dtype=I32)
    block_e = jnp.minimum(jnp.searchsorted(pad_end, blk * MOE_ROWS, side='right'), N_EXPERTS - 1).astype(I32)
    block_e = jnp.where(blk < n_used, block_e, block_e[n_used - 1])
    first = jnp.concatenate([jnp.ones((1,), bool), block_e[1:] != block_e[:-1]])
    block_run = jnp.cumsum(first.astype(I32)) - 1
    run_e = jnp.sum(jnp.where((block_run[None, :] == jnp.arange(N_EXPERTS, dtype=I32)[:, None]) & first[None, :],
                              block_e[None, :], 0), axis=1).astype(I32)
    meta = jnp.stack([n_used, block_run[-1] + 1]).astype(I32)
    return dest.astype(I32), src_tok, (block_e, block_run.astype(I32), run_e, meta)


def kernel(x_prompt, x_sample, cache_k, cache_v, page_table, c_prompt, c_sample, w_ada, b_ada, g_norm1, g_norm2, w_in, b_in, g_q, g_k, gmlp_ln_g, gmlp_ln_b, w_spatial, b_spatial, w_branch_a, w_branch_b, w_out, w_router, b_router, w_expert_in, b_expert_in, w_expert_out, b_expert_out):
    depth = w_ada.shape[0]
    assert depth == 1
    bp, sp, d = x_prompt.shape
    bs_, ss, _ = x_sample.shape
    n_pages = page_table.shape[1]
    page = cache_k.shape[2]
    past_len = n_pages * page
    mp, ms = bp * sp, bs_ * ss
    tm_p, tm_s = 512, ms

    c_all = jnp.concatenate([c_prompt, c_sample], axis=0)
    pad = (-c_all.shape[0]) % 16
    c_all = jnp.pad(c_all, ((0, pad), (0, 0)))
    mod = _adaln(c_all, w_ada[0], b_ada[0])
    mod_p = [mod[:bp, i * d:(i + 1) * d].reshape(bp, 1, d) for i in range(6)]
    mod_s = [jnp.broadcast_to(mod[bp:bp + bs_, None, i * d:(i + 1) * d], (bs_, ss, d)).reshape(1, ms, d)
             for i in range(6)]

    w_in_b = w_in[0].astype(BF16)
    cos_p, sin_p = _rope_tables(jnp.arange(sp, dtype=I32))
    cos_s, sin_s = _rope_tables(past_len + jnp.arange(ss, dtype=I32))
    cos_s, sin_s = jnp.tile(cos_s, (bs_, 1)), jnp.tile(sin_s, (bs_, 1))

    xp = x_prompt.reshape(mp, d)
    xs = x_sample.reshape(ms, d)
    proj = functools.partial(_in_proj, g1=g_norm1[0], w_in=w_in_b, b_in=b_in[0], g_q=g_q[0], g_k=g_k[0],
                             ln_g=gmlp_ln_g[0], ln_b=gmlp_ln_b[0])
    qp, kp, vp, up, gvp, sgap, sgbp = proj(xp, mod_p[1], mod_p[0], cos=cos_p, sin=sin_p, tm=tm_p)
    qs, ks, vs, us, gvs, sgas, sgbs = proj(xs, mod_s[1], mod_s[0], cos=cos_s, sin=sin_s, tm=tm_s)

    yb_p = _moba_prompt(qp.reshape(bp, sp, ATTN_WIDTH), kp.reshape(bp, sp, ATTN_WIDTH),
                        vp.reshape(bp, sp, ATTN_WIDTH)).reshape(mp, ATTN_WIDTH)
    yb_s = _moba_sample(qs.reshape(bs_, ss, ATTN_WIDTH), ks.reshape(bs_, ss, ATTN_WIDTH),
                        vs.reshape(bs_, ss, ATTN_WIDTH), cache_k.reshape(-1, HEAD_DIM),
                        cache_v.reshape(-1, HEAD_DIM), page_table, page).reshape(ms, ATTN_WIDTH)

    ws_p = w_spatial[0]
    bs_p = jnp.transpose(b_spatial[0])
    ws_s = jnp.einsum('ab,gts->gatbs', jnp.eye(bs_, dtype=F32), w_spatial[0][:, :ss, :ss]).reshape(
        GMLP_GROUPS, ms, ms)
    bs_s = jnp.tile(bs_p[:ss], (bs_, 1))
    w_a_b, w_b_b, w_o_b = w_branch_a[0].astype(BF16), w_branch_b[0].astype(BF16), w_out[0].astype(BF16)
    w_r_pad = jnp.pad(w_router[0], ((0, 0), (0, LANES - N_EXPERTS)))
    b_r_pad = jnp.pad(b_router[0], (0, LANES - N_EXPERTS)).reshape(1, LANES)
    mix = functools.partial(_mix_out, g2=g_norm2[0], w_a=w_a_b, w_b=w_b_b, w_o=w_o_b, w_r=w_r_pad, b_r=b_r_pad)
    x1p, h2p, idxp, gatep = mix(up, gvp, sgap, sgbp, yb_p, xp, mod_p[2], mod_p[4], mod_p[3],
                                ws=ws_p, bs_t=bs_p, tm=256)
    x1s, h2s, idxs, gates = mix(us, gvs, sgas, sgbs, yb_s, xs, mod_s[2], mod_s[4], mod_s[3],
                                ws=ws_s, bs_t=bs_s, tm=ms)

    h2 = jnp.concatenate([h2p, h2s], axis=0)
    top_idx = jnp.concatenate([idxp[:, :TOP_K], idxs[:, :TOP_K]], axis=0)
    n_tok = mp + ms
    n_assign = n_tok * TOP_K
    n_blocks = (n_assign + N_EXPERTS * (MOE_ROWS - 1) + MOE_ROWS - 1) // MOE_ROWS
    dest, src_tok, tables = _routing_tables(top_idx, n_blocks)
    x_sorted = _gather_rows(h2, src_tok, tables[3][:1], n_blocks)
    act = _expert_up(x_sorted, w_expert_in[0], b_expert_in[0], tables, hc=512)
    y_sorted = _expert_down(act, w_expert_out[0], b_expert_out[0], tables, oc=512)
    yp = _combine(y_sorted, dest, x1p, gatep, mod_p[5], tok0=0, tm=256)
    ys = _combine(y_sorted, dest, x1s, gates, mod_s[5], tok0=mp, tm=ms)

    shp = (depth, bp, sp, N_HEADS, HEAD_DIM)
    shs = (depth, bs_, ss, N_HEADS, HEAD_DIM)
    return (yp.reshape(bp, sp, d), ys.reshape(bs_, ss, d), kp.reshape(shp), vp.reshape(shp),
            ks.reshape(shs), vs.reshape(shs), gvs.reshape(depth, bs_, ss, gvs.shape[1]))
```

```python
import functools
import math

import jax
import jax.numpy as jnp
from jax import lax
from jax.experimental import pallas as pl
from jax.experimental.pallas import tpu as pltpu

F32 = jnp.float32
BF16 = jnp.bfloat16
I32 = jnp.int32
HIGHEST = lax.Precision.HIGHEST

N_HEADS = 8
HEAD_DIM = 128
ATTN_WIDTH = N_HEADS * HEAD_DIM
MOBA_BLOCK = 256
MOBA_TOPK = 3
ROPE_THETA = 10000.0
GMLP_GROUPS = 8
GMLP_GROUP_DIM = 128
CHUNK = 128
N_EXPERTS = 32
TOP_K = 4
SWIGLU_ALPHA = 1.702
SWIGLU_LIMIT = 7.0
EPS = 1e-6
NEG_BIG = -1e30

LANES = 128
MOE_ROWS = 256
PAGES_PER_STEP = 8
PAGE_SLOTS = 3
VMEM_LIMIT = 56 * 1024 * 1024


def _cparams(sem):
    return pltpu.CompilerParams(dimension_semantics=sem, vmem_limit_bytes=VMEM_LIMIT)


def _ada_kernel(c_ref, w_ref, b_ref, o_ref):
    c = c_ref[...]
    s = (c * jax.nn.sigmoid(c)).astype(BF16)
    o_ref[...] = jnp.dot(s, w_ref[...].astype(BF16), preferred_element_type=F32) + b_ref[...]


def _adaln(c, w_ada, b_ada):
    m, d = c.shape
    n = w_ada.shape[1]
    tn = 1024
    return pl.pallas_call(
        _ada_kernel,
        grid=(n // tn,),
        in_specs=[pl.BlockSpec((m, d), lambda j: (0, 0)),
                  pl.BlockSpec((d, tn), lambda j: (0, j)),
                  pl.BlockSpec((1, tn), lambda j: (0, j))],
        out_specs=pl.BlockSpec((m, tn), lambda j: (0, j)),
        out_shape=jax.ShapeDtypeStruct((m, n), F32),
        compiler_params=_cparams(("arbitrary",)),
        name="adaln",
    )(c, w_ada, b_ada.reshape(1, n))


def _gelu(x):
    return 0.5 * x * (1.0 + lax.erf(x * (1.0 / math.sqrt(2.0))))


def _in_kernel(x_ref, sc_ref, sh_ref, g1_ref, w_ref, b_ref, gq_ref, gk_ref, cos_ref, sin_ref, lng_ref, lnb_ref,
               q_ref, k_ref, v_ref, u_ref, gv_ref, sga_ref, sgb_ref, h_scr):
    j = pl.program_id(1)

    @pl.when(j == 0)
    def _():
        x = x_ref[...]
        r = lax.rsqrt(jnp.mean(x * x, axis=-1, keepdims=True) + EPS)
        h = (x * r * g1_ref[...]) * (1.0 + sc_ref[0]) + sh_ref[0]
        h_scr[...] = h.astype(BF16)

    z = jnp.dot(h_scr[...], w_ref[...], preferred_element_type=F32) + b_ref[...]

    def qk_post(g_ref, o_ref):
        cos = cos_ref[...]
        sin = sin_ref[...]
        g = g_ref[...]
        for hh in range(N_HEADS):
            zh = z[:, hh * HEAD_DIM:(hh + 1) * HEAD_DIM]
            r = lax.rsqrt(jnp.mean(zh * zh, axis=-1, keepdims=True) + EPS)
            y = zh * r * g
            o_ref[:, hh * HEAD_DIM:(hh + 1) * HEAD_DIM] = y * cos + pltpu.roll(y, HEAD_DIM // 2, 1) * sin

    @pl.when(j == 0)
    def _():
        qk_post(gq_ref, q_ref)

    @pl.when(j == 1)
    def _():
        qk_post(gk_ref, k_ref)

    @pl.when(j == 2)
    def _():
        v_ref[...] = z

    @pl.when(j == 3)
    def _():
        u_ref[...] = _gelu(z).astype(u_ref.dtype)

    @pl.when(j == 4)
    def _():
        a = _gelu(z)
        mu = jnp.mean(a, axis=-1, keepdims=True)
        d = a - mu
        var = jnp.mean(d * d, axis=-1, keepdims=True)
        gv_ref[...] = d * lax.rsqrt(var + EPS) * lng_ref[...] + lnb_ref[...]

    @pl.when((j == 5) | (j == 6))
    def _():
        sga_ref[...] = jax.nn.sigmoid(z).astype(sga_ref.dtype)

    @pl.when(j >= 7)
    def _():
        sgb_ref[...] = jax.nn.sigmoid(z).astype(sgb_ref.dtype)


def _in_proj(x, sc, sh, g1, w_in, b_in, g_q, g_k, cos, sin, ln_g, ln_b, tm):
    m, d = x.shape
    n = w_in.shape[1]
    tn = 1024
    assert n == 9 * tn and m % tm == 0
    nb, r, _ = sc.shape
    tiles_per_mod = m // (nb * tm)
    tiles_per_tab = cos.shape[0] // tm
    row = lambda i, j: (i, 0)
    mod_spec = pl.BlockSpec((1, r, d), lambda i, j: (i // tiles_per_mod, 0, 0))
    vec_d = pl.BlockSpec((1, d), lambda i, j: (0, 0))
    vec_h = pl.BlockSpec((1, HEAD_DIM), lambda i, j: (0, 0))
    vec_n = pl.BlockSpec((1, tn), lambda i, j: (0, 0))
    tab = pl.BlockSpec((tm, HEAD_DIM), lambda i, j: (i % tiles_per_tab, 0))
    outs = pl.pallas_call(
        _in_kernel,
        grid=(m // tm, n // tn),
        in_specs=[pl.BlockSpec((tm, d), row), mod_spec, mod_spec, vec_d,
                  pl.BlockSpec((d, tn), lambda i, j: (0, j)),
                  pl.BlockSpec((1, tn), lambda i, j: (0, j)),
                  vec_h, vec_h, tab, tab, vec_n, vec_n],
        out_specs=[pl.BlockSpec((tm, tn), row), pl.BlockSpec((tm, tn), row), pl.BlockSpec((tm, tn), row),
                   pl.BlockSpec((tm, tn), row), pl.BlockSpec((tm, tn), row),
                   pl.BlockSpec((tm, tn), lambda i, j: (i, jnp.clip(j - 5, 0, 1))),
                   pl.BlockSpec((tm, tn), lambda i, j: (i, jnp.clip(j - 7, 0, 1)))],
        out_shape=[jax.ShapeDtypeStruct((m, tn), F32), jax.ShapeDtypeStruct((m, tn), F32),
                   jax.ShapeDtypeStruct((m, tn), F32), jax.ShapeDtypeStruct((m, tn), BF16),
                   jax.ShapeDtypeStruct((m, tn), F32), jax.ShapeDtypeStruct((m, 2 * tn), BF16),
                   jax.ShapeDtypeStruct((m, 2 * tn), BF16)],
        scratch_shapes=[pltpu.VMEM((tm, d), BF16)],
        compiler_params=_cparams(("arbitrary", "arbitrary")),
        name="in_proj",
    )(x, sc, sh, g1.reshape(1, d), w_in, b_in.reshape(1, n), g_q.reshape(1, HEAD_DIM), g_k.reshape(1, HEAD_DIM),
      cos, sin, ln_g.reshape(1, tn), ln_b.reshape(1, tn))
    return outs


def _moba_prompt_kernel(q_ref, k_ref, v_ref, o_ref):
    s_len = q_ref.shape[1]
    nb = s_len // MOBA_BLOCK
    blk = MOBA_BLOCK
    q = q_ref[0]
    k = k_ref[0]
    means = jnp.concatenate(
        [jnp.mean(k[n * blk:(n + 1) * blk], axis=0, keepdims=True) for n in range(nb)], axis=0)
    gate = lax.dot_general(means, q, (((1,), (1,)), ((), ())), precision=HIGHEST,
                           preferred_element_type=F32)
    n_iota = lax.broadcasted_iota(I32, (nb, s_len), 0)
    own = lax.broadcasted_iota(I32, (nb, s_len), 1) // blk
    cnt = jnp.zeros((nb, s_len), I32)
    for m in range(nb):
        gm = gate[m:m + 1, :]
        beats = (m < own) & ((gm > gate) | ((gm == gate) & (m < n_iota)))
        cnt = cnt + beats.astype(I32)
    sel = ((n_iota < own) & (cnt < MOBA_TOPK)).astype(F32)
    sel = jnp.concatenate([sel, jnp.zeros((LANES - nb, s_len), F32)], axis=0)

    qb = q.astype(BF16)
    kb = k.astype(BF16)
    vb = v_ref[0].astype(BF16)
    scale = HEAD_DIM ** -0.5
    row = lax.broadcasted_iota(I32, (blk, blk), 0)
    col = lax.broadcasted_iota(I32, (blk, blk), 1)
    for i in range(nb):
        w = (i + 1) * blk
        s = lax.dot_general(qb[i * blk:(i + 1) * blk], kb[:w], (((1,), (1,)), ((), ())),
                            preferred_element_type=F32) * scale
        sel_t = sel[:, i * blk:(i + 1) * blk].T
        parts = [jnp.broadcast_to(sel_t[:, jj:jj + 1], (blk, blk)) for jj in range(i)]
        parts.append(jnp.where(col <= row, 1.0, 0.0))
        allowed = jnp.concatenate(parts, axis=1) if i else parts[0]
        s = jnp.where(allowed > 0.0, s, NEG_BIG)
        mx = jnp.max(s, axis=1, keepdims=True)
        p = jnp.exp(s - mx)
        l = jnp.sum(p, axis=1, keepdims=True)
        o = jnp.dot(p.astype(BF16), vb[:w], preferred_element_type=F32) / l
        o_ref[0, i * blk:(i + 1) * blk, :] = o.astype(o_ref.dtype)


def _moba_prompt(q, k, v):
    b, s, _ = q.shape
    assert s % MOBA_BLOCK == 0
    spec = pl.BlockSpec((1, s, HEAD_DIM), lambda bi, hi: (bi, 0, hi))
    return pl.pallas_call(
        _moba_prompt_kernel,
        grid=(b, N_HEADS),
        in_specs=[spec, spec, spec],
        out_specs=spec,
        out_shape=jax.ShapeDtypeStruct(q.shape, BF16),
        compiler_params=_cparams(("arbitrary", "arbitrary")),
        name="moba_prompt",
    )(q, k, v)


def _moba_sample_kernel(pt_ref, q_ref, kn_ref, vn_ref, kc_hbm, vc_hbm, o_ref,
                        buf, sem, qbd_scr, sc_scr, p_scr, means_scr, bmax_scr, ebig_scr, selx_scr,
                        pown_scr, l_scr, acc_scr, *, n_pages, n_req):
    pg = PAGES_PER_STEP
    ns = n_pages // pg
    nblk = n_pages // 2
    n_tok = kn_ref.shape[1]
    rows = N_HEADS * n_tok
    page_rows = buf.shape[2]
    r = pl.program_id(0)
    c = pl.program_id(1)
    step = r * (2 * ns) + c
    total = n_req * 2 * ns
    slot = step % PAGE_SLOTS
    scale = HEAD_DIM ** -0.5

    def page_copy(st, sl, p):
        r_ = st // (2 * ns)
        c_ = st % (2 * ns)
        page = pt_ref[r_ * n_pages + (c_ % ns) * pg + p]
        src = pl.ds(pl.multiple_of(page * page_rows, page_rows), page_rows)
        return c_ < ns, (pltpu.make_async_copy(kc_hbm.at[src, :], buf.at[sl, p], sem.at[sl, p]),
                         pltpu.make_async_copy(vc_hbm.at[src, :], buf.at[sl, p], sem.at[sl, p]))

    def fetch(st, sl, wait):
        for p in range(pg):
            is_k, (ck, cv) = page_copy(st, sl, p)

            @pl.when(is_k)
            def _():
                ck.wait() if wait else ck.start()

            @pl.when(jnp.logical_not(is_k))
            def _():
                cv.wait() if wait else cv.start()

    def load_page(p):
        return jnp.concatenate([buf[slot, p, pl.ds(hh, LANES, stride=N_HEADS), :] for hh in range(N_HEADS)],
                               axis=1)

    ahead = PAGE_SLOTS - 1

    @pl.when(step == 0)
    def _():
        for a in range(ahead):
            fetch(step + a, (step + a) % PAGE_SLOTS, False)
        rb = lax.broadcasted_iota(I32, ebig_scr.shape, 0)
        cb = lax.broadcasted_iota(I32, ebig_scr.shape, 1) // LANES
        ebig_scr[...] = jnp.where(rb == cb, 1.0, 0.0).astype(BF16)

    @pl.when(step + ahead < total)
    def _():
        fetch(step + ahead, (step + ahead) % PAGE_SLOTS, False)

    fetch(step, slot, True)

    @pl.when(c == 0)
    def _():
        q8 = q_ref[0]
        qrep = jnp.concatenate([q8] * N_HEADS, axis=0)
        rh = lax.broadcasted_iota(I32, qrep.shape, 0) // n_tok
        ch = lax.broadcasted_iota(I32, qrep.shape, 1) // HEAD_DIM
        qbd_scr[...] = jnp.where(rh == ch, qrep, 0.0)
        bmax_scr[...] = jnp.full(bmax_scr.shape, NEG_BIG, F32)

    @pl.when(c < ns)
    def _():
        qbd = qbd_scr[...].astype(BF16)
        lane = lax.broadcasted_iota(I32, bmax_scr.shape, 1)
        prev_sum = prev_s = None
        for p in range(pg):
            kp = load_page(p)
            s = lax.dot_general(qbd, kp.astype(BF16), (((1,), (1,)), ((), ())), preferred_element_type=F32)
            b_idx = (c * pg + p) // 2
            half = p % 2
            sc_scr[b_idx, :, half * LANES:(half + 1) * LANES] = s
            psum = jnp.sum(kp, axis=0, keepdims=True)
            if half == 0:
                prev_sum, prev_s = psum, s
            else:
                means_scr[pl.ds(b_idx, 1), :] = (prev_sum + psum) * (1.0 / MOBA_BLOCK)
                bm = jnp.max(jnp.maximum(prev_s, s), axis=1, keepdims=True)
                bmax_scr[...] = jnp.where(lane == b_idx, bm, bmax_scr[...])

    @pl.when(c == ns - 1)
    def _():
        qbd = qbd_scr[...]
        gate = lax.dot_general(qbd, means_scr[...], (((1,), (1,)), ((), ())), precision=HIGHEST,
                               preferred_element_type=F32)
        lane = lax.broadcasted_iota(I32, gate.shape, 1).astype(F32)
        g = gate
        sel = jnp.zeros(gate.shape, F32)
        for _ in range(MOBA_TOPK):
            mx = jnp.max(g, axis=1, keepdims=True)
            idx = jnp.min(jnp.where(g == mx, lane, float(nblk)), axis=1, keepdims=True)
            pick = lane == idx
            sel = jnp.where(pick, 1.0, sel)
            g = jnp.where(pick, -jnp.inf, g)
        selx_scr[...] = jnp.dot(sel.astype(BF16), ebig_scr[...], preferred_element_type=F32)

        kn = jnp.concatenate([kn_ref[0], jnp.zeros((LANES - n_tok, kn_ref.shape[2]), F32)], axis=0)
        s_own = lax.dot_general(qbd.astype(BF16), kn.astype(BF16), (((1,), (1,)), ((), ())),
                                preferred_element_type=F32) * scale
        t_q = lax.broadcasted_iota(I32, s_own.shape, 0) % n_tok
        t_k = lax.broadcasted_iota(I32, s_own.shape, 1)
        s_own = jnp.where(t_k <= t_q, s_own, NEG_BIG)

        m_sel = jnp.max(jnp.where(sel > 0.0, bmax_scr[:, :nblk] * scale, NEG_BIG), axis=1, keepdims=True)
        m = jnp.maximum(m_sel, jnp.max(s_own, axis=1, keepdims=True))
        m_b = jnp.broadcast_to(m, (rows, LANES))

        def exp_body(b_idx, lacc):
            off = pl.multiple_of(b_idx * LANES, LANES)
            msk = selx_scr[:, pl.ds(off, LANES)] > 0.0
            s = sc_scr[b_idx]
            p0 = jnp.where(msk, jnp.exp(s[:, :LANES] * scale - m_b), 0.0)
            p1 = jnp.where(msk, jnp.exp(s[:, LANES:] * scale - m_b), 0.0)
            p_scr[b_idx, :, :LANES] = p0.astype(BF16)
            p_scr[b_idx, :, LANES:] = p1.astype(BF16)
            return lacc + p0 + p1

        p_own = jnp.exp(s_own - m_b)
        lacc = lax.fori_loop(0, nblk, exp_body, p_own, unroll=4)
        pown_scr[...] = p_own.astype(BF16)
        l_scr[...] = jnp.broadcast_to(jnp.sum(lacc, axis=1, keepdims=True), l_scr.shape)

    @pl.when(c == ns)
    def _():
        vn = jnp.concatenate([vn_ref[0], jnp.zeros((LANES - n_tok, vn_ref.shape[2]), F32)], axis=0)
        acc_scr[...] = jnp.dot(pown_scr[...], vn.astype(BF16), preferred_element_type=F32)

    @pl.when(c >= ns)
    def _():
        acc = acc_scr[...]
        for p in range(pg):
            vp = load_page(p).astype(BF16)
            b_idx = ((c - ns) * pg + p) // 2
            half = p % 2
            acc = acc + jnp.dot(p_scr[b_idx, :, half * LANES:(half + 1) * LANES], vp, preferred_element_type=F32)
        acc_scr[...] = acc

    @pl.when(c == 2 * ns - 1)
    def _():
        o = acc_scr[...] / l_scr[:, 0:1]
        for hh in range(N_HEADS):
            o_ref[0, :, hh * HEAD_DIM:(hh + 1) * HEAD_DIM] = o[hh * n_tok:(hh + 1) * n_tok,
                                                               hh * HEAD_DIM:(hh + 1) * HEAD_DIM].astype(o_ref.dtype)


def _moba_sample(q, k_new, v_new, cache_k, cache_v, page_table, page):
    n_req, n_tok, width = q.shape
    n_pages = page_table.shape[1]
    assert page == LANES and 2 * page == MOBA_BLOCK and n_pages % (2 * PAGES_PER_STEP) == 0
    assert n_tok <= LANES and (N_HEADS * n_tok) % 8 == 0 and n_pages // 2 <= LANES
    ns = n_pages // PAGES_PER_STEP
    nblk = n_pages // 2
    rows = N_HEADS * n_tok
    tok_spec = pl.BlockSpec((1, n_tok, width), lambda r, c, pt: (r, 0, 0))
    kern = functools.partial(_moba_sample_kernel, n_pages=n_pages, n_req=n_req)
    return pl.pallas_call(
        kern,
        grid_spec=pltpu.PrefetchScalarGridSpec(
            num_scalar_prefetch=1,
            grid=(n_req, 2 * ns),
            in_specs=[tok_spec, tok_spec, tok_spec,
                      pl.BlockSpec(memory_space=pl.ANY), pl.BlockSpec(memory_space=pl.ANY)],
            out_specs=tok_spec,
            scratch_shapes=[pltpu.VMEM((PAGE_SLOTS, PAGES_PER_STEP, page * N_HEADS, HEAD_DIM), F32),
                            pltpu.SemaphoreType.DMA((PAGE_SLOTS, PAGES_PER_STEP)),
                            pltpu.VMEM((rows, width), F32),
                            pltpu.VMEM((nblk, rows, MOBA_BLOCK), F32),
                            pltpu.VMEM((nblk, rows, MOBA_BLOCK), BF16),
                            pltpu.VMEM((nblk, width), F32),
                            pltpu.VMEM((rows, LANES), F32),
                            pltpu.VMEM((nblk, nblk * LANES), BF16),
                            pltpu.VMEM((rows, nblk * LANES), F32),
                            pltpu.VMEM((rows, LANES), BF16),
                            pltpu.VMEM((rows, LANES), F32),
                            pltpu.VMEM((rows, width), F32)]),
        out_shape=jax.ShapeDtypeStruct(q.shape, BF16),
        compiler_params=_cparams(("arbitrary", "arbitrary")),
        name="moba_sample",
    )(page_table.reshape(-1), q, k_new, v_new, cache_k, cache_v)


def _mix_out_kernel(u_ref, gv_ref, sga_ref, sgb_ref, yb_ref, x_ref, gt1_ref, sc2_ref, sh2_ref, g2_ref,
                    ws_ref, bs_ref, wa_ref, wb_ref, wo_ref, wr_ref, br_ref,
                    x1_ref, h2_ref, idx_ref, gate_ref, ya_scr):
    tm = u_ref.shape[0]
    cs = ws_ref.shape[1]
    row = lax.broadcasted_iota(I32, (cs, cs), 0)
    col = lax.broadcasted_iota(I32, (cs, cs), 1)
    causal = col <= row
    bs = bs_ref[...]
    for g in range(GMLP_GROUPS):
        ws = jnp.where(causal, ws_ref[g], 0.0).astype(BF16)
        gs = slice(g * GMLP_GROUP_DIM, (g + 1) * GMLP_GROUP_DIM)
        for ci in range(tm // cs):
            rs = slice(ci * cs, (ci + 1) * cs)
            mixed = jnp.dot(ws, gv_ref[rs, gs].astype(BF16), preferred_element_type=F32) + bs[:, g:g + 1]
            ya_scr[rs, gs] = (u_ref[rs, gs].astype(F32) * mixed).astype(BF16)

    a = jnp.dot(ya_scr[...], wa_ref[...], preferred_element_type=F32)
    b = jnp.dot(yb_ref[...], wb_ref[...], preferred_element_type=F32)
    merged = sga_ref[...].astype(F32) * a + sgb_ref[...].astype(F32) * b
    mix = jnp.dot(merged.astype(BF16), wo_ref[...], preferred_element_type=F32)
    x1 = x_ref[...] + gt1_ref[0] * mix
    x1_ref[...] = x1
    r = lax.rsqrt(jnp.mean(x1 * x1, axis=-1, keepdims=True) + EPS)
    h2 = (x1 * r * g2_ref[...]) * (1.0 + sc2_ref[0]) + sh2_ref[0]
    h2_ref[...] = h2

    h_hi = h2.astype(BF16)
    h_lo = (h2 - h_hi.astype(F32)).astype(BF16)
    w_hi = wr_ref[0]
    logits = (jnp.dot(h_hi, w_hi, preferred_element_type=F32) + jnp.dot(h_lo, w_hi, preferred_element_type=F32)
              + jnp.dot(h_hi, wr_ref[1], preferred_element_type=F32)) + br_ref[...]
    lane = lax.broadcasted_iota(I32, logits.shape, 1)
    lane_f = lane.astype(F32)
    g = jnp.where(lane < N_EXPERTS, logits, -jnp.inf)
    vals, idxs = [], []
    for _ in range(TOP_K):
        mx = jnp.max(g, axis=1, keepdims=True)
        idx = jnp.min(jnp.where(g == mx, lane_f, float(LANES)), axis=1, keepdims=True)
        vals.append(mx)
        idxs.append(idx.astype(I32))
        g = jnp.where(lane_f == idx, -jnp.inf, g)
    es = [jnp.exp(v - vals[0]) for v in vals]
    den = es[0] + es[1] + es[2] + es[3]
    idx_out = jnp.zeros(logits.shape, I32)
    gate_out = jnp.zeros(logits.shape, F32)
    for kk in range(TOP_K):
        idx_out = jnp.where(lane == kk, idxs[kk], idx_out)
        gate_out = jnp.where(lane == kk, es[kk] / den, gate_out)
    idx_ref[...] = idx_out
    gate_ref[...] = gate_out


def _mix_out(u, gv, sga, sgb, yb, x, gt1, sc2, sh2, g2, ws, bs_t, w_a, w_b, w_o, w_r, b_r, tm):
    m, d = x.shape
    nb, r, _ = gt1.shape
    tiles_per_mod = m // (nb * tm)
    cs = ws.shape[1]
    assert m % tm == 0 and tm % cs == 0
    gw = u.shape[1]
    row = lambda i: (i, 0)
    const2 = lambda i: (0, 0)
    mod_spec = pl.BlockSpec((1, r, d), lambda i: (i // tiles_per_mod, 0, 0))
    resident = dict(pipeline_mode=pl.Buffered(1))
    return pl.pallas_call(
        _mix_out_kernel,
        grid=(m // tm,),
        in_specs=[pl.BlockSpec((tm, gw), row), pl.BlockSpec((tm, gw), row),
                  pl.BlockSpec((tm, d), row), pl.BlockSpec((tm, d), row),
                  pl.BlockSpec((tm, ATTN_WIDTH), row), pl.BlockSpec((tm, d), row),
                  mod_spec, mod_spec, mod_spec,
                  pl.BlockSpec((1, d), const2),
                  pl.BlockSpec(ws.shape, lambda i: (0, 0, 0), **resident),
                  pl.BlockSpec(bs_t.shape, const2),
                  pl.BlockSpec(w_a.shape, const2, **resident),
                  pl.BlockSpec(w_b.shape, const2, **resident),
                  pl.BlockSpec(w_o.shape, const2, **resident),
                  pl.BlockSpec(w_r.shape, lambda i: (0, 0, 0), **resident),
                  pl.BlockSpec((1, LANES), const2)],
        out_specs=[pl.BlockSpec((tm, d), row), pl.BlockSpec((tm, d), row),
                   pl.BlockSpec((tm, LANES), row), pl.BlockSpec((tm, LANES), row)],
        out_shape=[jax.ShapeDtypeStruct((m, d), F32), jax.ShapeDtypeStruct((m, d), F32),
                   jax.ShapeDtypeStruct((m, LANES), I32), jax.ShapeDtypeStruct((m, LANES), F32)],
        scratch_shapes=[pltpu.VMEM((tm, gw), BF16)],
        compiler_params=_cparams(("arbitrary",)),
        name="mix_out",
    )(u, gv, sga, sgb, yb, x, gt1, sc2, sh2, g2.reshape(1, d), ws, bs_t, w_a, w_b, w_o, w_r, b_r)


def _gather_kernel(src_ref, nused_ref, h_hbm, o_ref, buf, sem):
    rb = pl.program_id(0)
    tb = o_ref.shape[0]
    slot = rb % 2
    nused = nused_ref[0]

    def row_copy(blk, sl, i):
        tok = src_ref[blk * tb + i]
        return pltpu.make_async_copy(h_hbm.at[pl.ds(tok, 1), :], buf.at[sl, pl.ds(i, 1), :], sem.at[sl])

    def start_block(blk, sl):
        def body(i, carry):
            row_copy(blk, sl, i).start()
            return carry
        lax.fori_loop(0, tb, body, 0, unroll=8)

    @pl.when(rb == 0)
    def _():
        start_block(rb, slot)

    @pl.when(rb + 1 < nused)
    def _():
        start_block(rb + 1, 1 - slot)

    @pl.when(rb < nused)
    def _():
        pltpu.make_async_copy(h_hbm.at[pl.ds(0, tb), :], buf.at[slot], sem.at[slot]).wait()
        o_ref[...] = buf[slot].astype(o_ref.dtype)

    @pl.when(rb >= nused)
    def _():
        o_ref[...] = jnp.zeros(o_ref.shape, o_ref.dtype)


def _gather_rows(h2, src_tok, n_used, n_blocks):
    d = h2.shape[1]
    return pl.pallas_call(
        _gather_kernel,
        grid_spec=pltpu.PrefetchScalarGridSpec(
            num_scalar_prefetch=2,
            grid=(n_blocks,),
            in_specs=[pl.BlockSpec(memory_space=pl.ANY)],
            out_specs=pl.BlockSpec((MOE_ROWS, d), lambda rb, src, nu: (rb, 0)),
            scratch_shapes=[pltpu.VMEM((2, MOE_ROWS, d), F32), pltpu.SemaphoreType.DMA((2,))]),
        out_shape=jax.ShapeDtypeStruct((n_blocks * MOE_ROWS, d), BF16),
        compiler_params=_cparams(("arbitrary",)),
        name="moe_gather",
    )(src_tok, n_used, h2)


def _stream_weights(brun_ref, rune_ref, meta_ref, w_hbm, wbuf, sem, w_scrs, part_bases, width):
    n = pl.program_id(0)
    rb = pl.program_id(1)
    n_runs = meta_ref[1]
    run = brun_ref[rb]
    g = n * n_runs + run
    first = (rb == 0) | (brun_ref[jnp.maximum(rb - 1, 0)] != run)

    def copies(gg):
        e = rune_ref[gg % n_runs]
        nn = gg // n_runs
        sl = gg % 2
        return [pltpu.make_async_copy(
            w_hbm.at[e, :, pl.ds(pl.multiple_of(base + nn * width, width), width)], wbuf.at[sl, part], sem.at[sl, part])
            for part, base in enumerate(part_bases)]

    @pl.when(first)
    def _():
        @pl.when(g == 0)
        def _():
            for cp in copies(g):
                cp.start()

        @pl.when(g + 1 < pl.num_programs(0) * n_runs)
        def _():
            for cp in copies(g + 1):
                cp.start()

        for cp in copies(g):
            cp.wait()
        for part, scr in enumerate(w_scrs):
            scr[...] = wbuf[g % 2, part].astype(BF16)


def _up_kernel(be_ref, brun_ref, rune_ref, meta_ref, x_ref, bg_ref, bl_ref, w_hbm, o_ref,
               wbuf, sem, wg_scr, wl_scr, *, de):
    rb = pl.program_id(1)
    n_used = meta_ref[0]
    _stream_weights(brun_ref, rune_ref, meta_ref, w_hbm, wbuf, sem, (wg_scr, wl_scr), (0, de), wg_scr.shape[1])

    @pl.when(rb < n_used)
    def _():
        x = x_ref[...]
        glu = jnp.dot(x, wg_scr[...], preferred_element_type=F32) + bg_ref[0]
        lin = jnp.dot(x, wl_scr[...], preferred_element_type=F32) + bl_ref[0]
        glu = jnp.minimum(glu, SWIGLU_LIMIT)
        lin = jnp.clip(lin, -SWIGLU_LIMIT, SWIGLU_LIMIT)
        o_ref[...] = (glu * jax.nn.sigmoid(SWIGLU_ALPHA * glu) * (lin + 1.0)).astype(o_ref.dtype)

    @pl.when(rb >= n_used)
    def _():
        o_ref[...] = jnp.zeros(o_ref.shape, o_ref.dtype)


def _expert_up(xs, w1, b1, tables, hc):
    p, d = xs.shape
    de = w1.shape[2] // 2
    n_blocks = p // MOE_ROWS
    nh = de // hc
    b1r = b1.reshape(N_EXPERTS, 1, 2 * de)
    return pl.pallas_call(
        functools.partial(_up_kernel, de=de),
        grid_spec=pltpu.PrefetchScalarGridSpec(
            num_scalar_prefetch=4,
            grid=(nh, n_blocks),
            in_specs=[pl.BlockSpec((MOE_ROWS, d), lambda n, rb, be, br, re, mt: (jnp.minimum(rb, mt[0] - 1), 0)),
                      pl.BlockSpec((1, 1, hc), lambda n, rb, be, br, re, mt: (be[rb], 0, n)),
                      pl.BlockSpec((1, 1, hc), lambda n, rb, be, br, re, mt: (be[rb], 0, nh + n)),
                      pl.BlockSpec(memory_space=pl.ANY)],
            out_specs=pl.BlockSpec((MOE_ROWS, hc), lambda n, rb, be, br, re, mt: (rb, n)),
            scratch_shapes=[pltpu.VMEM((2, 2, d, hc), F32), pltpu.SemaphoreType.DMA((2, 2)),
                            pltpu.VMEM((d, hc), BF16), pltpu.VMEM((d, hc), BF16)]),
        out_shape=jax.ShapeDtypeStruct((p, de), BF16),
        compiler_params=_cparams(("arbitrary", "arbitrary")),
        name="expert_up",
    )(*tables, xs, b1r, b1r, w1)


def _down_kernel(be_ref, brun_ref, rune_ref, meta_ref, a_ref, b_ref, w_hbm, o_ref, wbuf, sem, w_scr):
    rb = pl.program_id(1)
    n_used = meta_ref[0]
    _stream_weights(brun_ref, rune_ref, meta_ref, w_hbm, wbuf, sem, (w_scr,), (0,), w_scr.shape[1])

    @pl.when(rb < n_used)
    def _():
        o_ref[...] = jnp.dot(a_ref[...], w_scr[...], preferred_element_type=F32) + b_ref[0]

    @pl.when(rb >= n_used)
    def _():
        o_ref[...] = jnp.zeros(o_ref.shape, o_ref.dtype)


def _expert_down(act, w2, b2, tables, oc):
    p, de = act.shape
    d = w2.shape[2]
    n_blocks = p // MOE_ROWS
    b2r = b2.reshape(N_EXPERTS, 1, d)
    return pl.pallas_call(
        _down_kernel,
        grid_spec=pltpu.PrefetchScalarGridSpec(
            num_scalar_prefetch=4,
            grid=(d // oc, n_blocks),
            in_specs=[pl.BlockSpec((MOE_ROWS, de), lambda n, rb, be, br, re, mt: (jnp.minimum(rb, mt[0] - 1), 0)),
                      pl.BlockSpec((1, 1, oc), lambda n, rb, be, br, re, mt: (be[rb], 0, n)),
                      pl.BlockSpec(memory_space=pl.ANY)],
            out_specs=pl.BlockSpec((MOE_ROWS, oc), lambda n, rb, be, br, re, mt: (rb, n)),
            scratch_shapes=[pltpu.VMEM((2, 1, de, oc), F32), pltpu.SemaphoreType.DMA((2, 1)),
                            pltpu.VMEM((de, oc), BF16)]),
        out_shape=jax.ShapeDtypeStruct((p, d), F32),
        compiler_params=_cparams(("arbitrary", "arbitrary")),
        name="expert_down",
    )(*tables, act, b2r, w2)


def _combine_kernel(dest_ref, y_hbm, x1_ref, gate_ref, gt2_ref, o_ref, buf, sem, *, tok0):
    i = pl.program_id(0)
    n = pl.num_programs(0)
    tm = o_ref.shape[0]
    slot = i % 2

    def row_copy(tile, sl, t, kk):
        src = dest_ref[(tok0 + tile * tm + t) * TOP_K + kk]
        return pltpu.make_async_copy(y_hbm.at[pl.ds(src, 1), :], buf.at[sl, kk, pl.ds(t, 1), :], sem.at[sl])

    def start_rows(tile, sl):
        def body(t, carry):
            for kk in range(TOP_K):
                row_copy(tile, sl, t, kk).start()
            return carry
        lax.fori_loop(0, tm, body, 0, unroll=2)

    @pl.when(i == 0)
    def _():
        start_rows(i, slot)

    @pl.when(i + 1 < n)
    def _():
        start_rows(i + 1, 1 - slot)

    for kk in range(TOP_K):
        pltpu.make_async_copy(y_hbm.at[pl.ds(0, tm), :], buf.at[slot, kk], sem.at[slot]).wait()
    gates = gate_ref[...]
    ff = gates[:, 0:1] * buf[slot, 0]
    for kk in range(1, TOP_K):
        ff = ff + gates[:, kk:kk + 1] * buf[slot, kk]
    o_ref[...] = x1_ref[...] + gt2_ref[0] * ff


def _combine(y_sorted, dest, x1, gates, gt2, tok0, tm):
    m, d = x1.shape
    nb, r, _ = gt2.shape
    tiles_per_mod = m // (nb * tm)
    kern = functools.partial(_combine_kernel, tok0=tok0)
    return pl.pallas_call(
        kern,
        grid_spec=pltpu.PrefetchScalarGridSpec(
            num_scalar_prefetch=1,
            grid=(m // tm,),
            in_specs=[pl.BlockSpec(memory_space=pl.ANY),
                      pl.BlockSpec((tm, d), lambda i, de: (i, 0)),
                      pl.BlockSpec((tm, LANES), lambda i, de: (i, 0)),
                      pl.BlockSpec((1, r, d), lambda i, de: (i // tiles_per_mod, 0, 0))],
            out_specs=pl.BlockSpec((tm, d), lambda i, de: (i, 0)),
            scratch_shapes=[pltpu.VMEM((2, TOP_K, tm, d), F32), pltpu.SemaphoreType.DMA((2,))]),
        out_shape=jax.ShapeDtypeStruct((m, d), F32),
        compiler_params=_cparams(("arbitrary",)),
        name="moe_combine",
    )(dest, y_sorted, x1, gates, gt2)


def _rope_tables(pos):
    half = HEAD_DIM // 2
    inv_freq = ROPE_THETA ** (-jnp.arange(half, dtype=F32) / half)
    ang = pos.astype(F32)[:, None] * inv_freq[None, :]
    cos = jnp.cos(ang)
    sin = jnp.sin(ang)
    return jnp.concatenate([cos, cos], axis=1), jnp.concatenate([-sin, sin], axis=1)


def _routing_tables(top_idx, n_blocks):
    e_flat = top_idx.reshape(-1)
    a = e_flat.shape[0]
    ck = LANES
    assert a % ck == 0
    onehot = e_flat[:, None] == jnp.arange(N_EXPERTS, dtype=I32)[None, :]
    within = jnp.einsum('ts,csn->ctn', jnp.tril(jnp.ones((ck, ck), BF16)), onehot.astype(BF16).reshape(a // ck, ck, -1),
                        preferred_element_type=F32)
    tot = within[:, -1, :]
    before = jnp.dot(jnp.tril(jnp.ones((a // ck, a // ck), F32), -1), tot, precision=HIGHEST)
    csum = (within + before[:, None, :]).reshape(a, N_EXPERTS).astype(I32)
    counts = csum[-1]
    rank = jnp.sum(jnp.where(onehot, csum, 0), axis=1) - 1
    padded = (counts + MOE_ROWS - 1) // MOE_ROWS * MOE_ROWS
    pad_end = jnp.cumsum(padded)
    pad_start = pad_end - padded
    dest = pad_start[e_flat] + rank
    src_tok = jnp.zeros((n_blocks * MOE_ROWS,), I32).at[dest].set(jnp.arange(a, dtype=I32) // TOP_K)
    n_used = pad_end[-1] // MOE_ROWS
    blk = jnp.arange(n_blocks, dtype=I32)
    block_e = jnp.minimum(jnp.sum((pad_end[None, :] <= blk[:, None] * MOE_ROWS).astype(I32), axis=1), N_EXPERTS - 1)
    block_e = jnp.where(blk < n_used, block_e, block_e[n_used - 1])
    first = jnp.concatenate([jnp.ones((1,), bool), block_e[1:] != block_e[:-1]])
    block_run = jnp.cumsum(first.astype(I32)) - 1
    run_e = jnp.sum(jnp.where((block_run[None, :] == jnp.arange(N_EXPERTS, dtype=I32)[:, None]) & first[None, :],
                              block_e[None, :], 0), axis=1).astype(I32)
    meta = jnp.stack([n_used, block_run[-1] + 1]).astype(I32)
    return dest.astype(I32), src_tok, (block_e, block_run.astype(I32), run_e, meta)


def kernel(x_prompt, x_sample, cache_k, cache_v, page_table, c_prompt, c_sample, w_ada, b_ada, g_norm1, g_norm2, w_in, b_in, g_q, g_k, gmlp_ln_g, gmlp_ln_b, w_spatial, b_spatial, w_branch_a, w_branch_b, w_out, w_router, b_router, w_expert_in, b_expert_in, w_expert_out, b_expert_out):
    depth = w_ada.shape[0]
    assert depth == 1
    bp, sp, d = x_prompt.shape
    bs_, ss, _ = x_sample.shape
    n_pages = page_table.shape[1]
    page = cache_k.shape[2]
    past_len = n_pages * page
    mp, ms = bp * sp, bs_ * ss
    tm_p, tm_s = 512, ms

    c_all = jnp.concatenate([c_prompt, c_sample], axis=0)
    pad = (-c_all.shape[0]) % 16
    c_all = jnp.pad(c_all, ((0, pad), (0, 0)))
    mod = _adaln(c_all, w_ada[0], b_ada[0])
    mod_p = [mod[:bp, i * d:(i + 1) * d].reshape(bp, 1, d) for i in range(6)]
    mod_s = [jnp.broadcast_to(mod[bp:bp + bs_, None, i * d:(i + 1) * d], (bs_, ss, d)).reshape(1, ms, d)
             for i in range(6)]

    w_in_b = w_in[0].astype(BF16)
    cos_p, sin_p = _rope_tables(jnp.arange(sp, dtype=I32))
    cos_s, sin_s = _rope_tables(past_len + jnp.arange(ss, dtype=I32))
    cos_s, sin_s = jnp.tile(cos_s, (bs_, 1)), jnp.tile(sin_s, (bs_, 1))

    xp = x_prompt.reshape(mp, d)
    xs = x_sample.reshape(ms, d)
    proj = functools.partial(_in_proj, g1=g_norm1[0], w_in=w_in_b, b_in=b_in[0], g_q=g_q[0], g_k=g_k[0],
                             ln_g=gmlp_ln_g[0], ln_b=gmlp_ln_b[0])
    qp, kp, vp, up, gvp, sgap, sgbp = proj(xp, mod_p[1], mod_p[0], cos=cos_p, sin=sin_p, tm=tm_p)
    qs, ks, vs, us, gvs, sgas, sgbs = proj(xs, mod_s[1], mod_s[0], cos=cos_s, sin=sin_s, tm=tm_s)

    yb_p = _moba_prompt(qp.reshape(bp, sp, ATTN_WIDTH), kp.reshape(bp, sp, ATTN_WIDTH),
                        vp.reshape(bp, sp, ATTN_WIDTH)).reshape(mp, ATTN_WIDTH)
    yb_s = _moba_sample(qs.reshape(bs_, ss, ATTN_WIDTH), ks.reshape(bs_, ss, ATTN_WIDTH),
                        vs.reshape(bs_, ss, ATTN_WIDTH), cache_k.reshape(-1, HEAD_DIM),
                        cache_v.reshape(-1, HEAD_DIM), page_table, page).reshape(ms, ATTN_WIDTH)

    ws_p = w_spatial[0]
    bs_p = jnp.transpose(b_spatial[0])
    tile_t = (jnp.arange(ms, dtype=I32)[:, None] % ss == jnp.arange(ss, dtype=I32)[None, :]).astype(F32)
    same_req = jnp.arange(ms, dtype=I32)[:, None] // ss == jnp.arange(ms, dtype=I32)[None, :] // ss
    ws_s = jnp.where(same_req[None], jnp.einsum('it,gts,js->gij', tile_t, w_spatial[0][:, :ss, :ss], tile_t,
                                                precision=HIGHEST), 0.0)
    bs_s = jnp.tile(bs_p[:ss], (bs_, 1))
    w_a_b, w_b_b, w_o_b = w_branch_a[0].astype(BF16), w_branch_b[0].astype(BF16), w_out[0].astype(BF16)
    w_r_pad = jnp.pad(w_router[0], ((0, 0), (0, LANES - N_EXPERTS)))
    w_r_hi = w_r_pad.astype(BF16)
    w_r_pad = jnp.stack([w_r_hi, (w_r_pad - w_r_hi.astype(F32)).astype(BF16)])
    b_r_pad = jnp.pad(b_router[0], (0, LANES - N_EXPERTS)).reshape(1, LANES)
    mix = functools.partial(_mix_out, g2=g_norm2[0], w_a=w_a_b, w_b=w_b_b, w_o=w_o_b, w_r=w_r_pad, b_r=b_r_pad)
    x1p, h2p, idxp, gatep = mix(up, gvp, sgap, sgbp, yb_p, xp, mod_p[2], mod_p[4], mod_p[3],
                                ws=ws_p, bs_t=bs_p, tm=256)
    x1s, h2s, idxs, gates = mix(us, gvs, sgas, sgbs, yb_s, xs, mod_s[2], mod_s[4], mod_s[3],
                                ws=ws_s, bs_t=bs_s, tm=ms)

    h2 = jnp.concatenate([h2p, h2s], axis=0)
    top_idx = jnp.concatenate([idxp[:, :TOP_K], idxs[:, :TOP_K]], axis=0)
    n_tok = mp + ms
    n_assign = n_tok * TOP_K
    n_blocks = (n_assign + N_EXPERTS * (MOE_ROWS - 1) + MOE_ROWS - 1) // MOE_ROWS
    dest, src_tok, tables = _routing_tables(top_idx, n_blocks)
    x_sorted = _gather_rows(h2, src_tok, tables[3][:1], n_blocks)
    act = _expert_up(x_sorted, w_expert_in[0], b_expert_in[0], tables, hc=1024)
    y_sorted = _expert_down(act, w_expert_out[0], b_expert_out[0], tables, oc=2048)
    yp = _combine(y_sorted, dest, x1p, gatep, mod_p[5], tok0=0, tm=256)
    ys = _combine(y_sorted, dest, x1s, gates, mod_s[5], tok0=mp, tm=ms)

    shp = (depth, bp, sp, N_HEADS, HEAD_DIM)
    shs = (depth, bs_, ss, N_HEADS, HEAD_DIM)
    return (yp.reshape(bp, sp, d), ys.reshape(bs_, ss, d), kp.reshape(shp), vp.reshape(shp),
            ks.reshape(shs), vs.reshape(shs), gvs.reshape(depth, bs_, ss, gvs.shape[1]))
```

```python
import functools
import math

import jax
import jax.numpy as jnp
from jax import lax
from jax.experimental import pallas as pl
from jax.experimental.pallas import tpu as pltpu

F32 = jnp.float32
BF16 = jnp.bfloat16
I32 = jnp.int32
HIGHEST = lax.Precision.HIGHEST

N_HEADS = 8
HEAD_DIM = 128
ATTN_WIDTH = N_HEADS * HEAD_DIM
MOBA_BLOCK = 256
MOBA_TOPK = 3
ROPE_THETA = 10000.0
GMLP_GROUPS = 8
GMLP_GROUP_DIM = 128
CHUNK = 128
N_EXPERTS = 32
TOP_K = 4
SWIGLU_ALPHA = 1.702
SWIGLU_LIMIT = 7.0
EPS = 1e-6
NEG_BIG = -1e30

LANES = 128
MOE_ROWS = 256
PAGES_PER_STEP = 8
PAGE_SLOTS = 3
VMEM_LIMIT = 56 * 1024 * 1024


def _cparams(sem):
    return pltpu.CompilerParams(dimension_semantics=sem, vmem_limit_bytes=VMEM_LIMIT)


def _ada_kernel(c_ref, w_ref, b_ref, o_ref):
    c = c_ref[...]
    s = (c * jax.nn.sigmoid(c)).astype(BF16)
    o_ref[...] = jnp.dot(s, w_ref[...].astype(BF16), preferred_element_type=F32) + b_ref[...]


def _adaln(c, w_ada, b_ada):
    m, d = c.shape
    n = w_ada.shape[1]
    tn = 1024
    return pl.pallas_call(
        _ada_kernel,
        grid=(n // tn,),
        in_specs=[pl.BlockSpec((m, d), lambda j: (0, 0)),
                  pl.BlockSpec((d, tn), lambda j: (0, j)),
                  pl.BlockSpec((1, tn), lambda j: (0, j))],
        out_specs=pl.BlockSpec((m, tn), lambda j: (0, j)),
        out_shape=jax.ShapeDtypeStruct((m, n), F32),
        compiler_params=_cparams(("arbitrary",)),
        name="adaln",
    )(c, w_ada, b_ada.reshape(1, n))


def _gelu(x):
    return 0.5 * x * (1.0 + lax.erf(x * (1.0 / math.sqrt(2.0))))


def _in_kernel(x_ref, sc_ref, sh_ref, g1_ref, w_ref, b_ref, gq_ref, gk_ref, cos_ref, sin_ref, lng_ref, lnb_ref,
               q_ref, k_ref, v_ref, u_ref, gv_ref, sga_ref, sgb_ref, h_scr):
    j = pl.program_id(1)

    @pl.when(j == 0)
    def _():
        x = x_ref[...]
        r = lax.rsqrt(jnp.mean(x * x, axis=-1, keepdims=True) + EPS)
        h = (x * r * g1_ref[...]) * (1.0 + sc_ref[0]) + sh_ref[0]
        h_scr[...] = h.astype(BF16)

    z = jnp.dot(h_scr[...], w_ref[...], preferred_element_type=F32) + b_ref[...]

    def qk_post(g_ref, o_ref):
        cos = cos_ref[...]
        sin = sin_ref[...]
        g = g_ref[...]
        for hh in range(N_HEADS):
            zh = z[:, hh * HEAD_DIM:(hh + 1) * HEAD_DIM]
            r = lax.rsqrt(jnp.mean(zh * zh, axis=-1, keepdims=True) + EPS)
            y = zh * r * g
            o_ref[:, hh * HEAD_DIM:(hh + 1) * HEAD_DIM] = y * cos + pltpu.roll(y, HEAD_DIM // 2, 1) * sin

    @pl.when(j == 0)
    def _():
        qk_post(gq_ref, q_ref)

    @pl.when(j == 1)
    def _():
        qk_post(gk_ref, k_ref)

    @pl.when(j == 2)
    def _():
        v_ref[...] = z

    @pl.when(j == 3)
    def _():
        u_ref[...] = _gelu(z).astype(u_ref.dtype)

    @pl.when(j == 4)
    def _():
        a = _gelu(z)
        mu = jnp.mean(a, axis=-1, keepdims=True)
        d = a - mu
        var = jnp.mean(d * d, axis=-1, keepdims=True)
        gv_ref[...] = d * lax.rsqrt(var + EPS) * lng_ref[...] + lnb_ref[...]

    @pl.when((j == 5) | (j == 6))
    def _():
        sga_ref[...] = jax.nn.sigmoid(z).astype(sga_ref.dtype)

    @pl.when(j >= 7)
    def _():
        sgb_ref[...] = jax.nn.sigmoid(z).astype(sgb_ref.dtype)


def _in_proj(x, sc, sh, g1, w_in, b_in, g_q, g_k, cos, sin, ln_g, ln_b, tm):
    m, d = x.shape
    n = w_in.shape[1]
    tn = 1024
    assert n == 9 * tn and m % tm == 0
    nb, r, _ = sc.shape
    tiles_per_mod = m // (nb * tm)
    tiles_per_tab = cos.shape[0] // tm
    row = lambda i, j: (i, 0)
    mod_spec = pl.BlockSpec((1, r, d), lambda i, j: (i // tiles_per_mod, 0, 0))
    vec_d = pl.BlockSpec((1, d), lambda i, j: (0, 0))
    vec_h = pl.BlockSpec((1, HEAD_DIM), lambda i, j: (0, 0))
    vec_n = pl.BlockSpec((1, tn), lambda i, j: (0, 0))
    tab = pl.BlockSpec((tm, HEAD_DIM), lambda i, j: (i % tiles_per_tab, 0))
    outs = pl.pallas_call(
        _in_kernel,
        grid=(m // tm, n // tn),
        in_specs=[pl.BlockSpec((tm, d), row), mod_spec, mod_spec, vec_d,
                  pl.BlockSpec((d, tn), lambda i, j: (0, j)),
                  pl.BlockSpec((1, tn), lambda i, j: (0, j)),
                  vec_h, vec_h, tab, tab, vec_n, vec_n],
        out_specs=[pl.BlockSpec((tm, tn), row), pl.BlockSpec((tm, tn), row), pl.BlockSpec((tm, tn), row),
                   pl.BlockSpec((tm, tn), row), pl.BlockSpec((tm, tn), row),
                   pl.BlockSpec((tm, tn), lambda i, j: (i, jnp.clip(j - 5, 0, 1))),
                   pl.BlockSpec((tm, tn), lambda i, j: (i, jnp.clip(j - 7, 0, 1)))],
        out_shape=[jax.ShapeDtypeStruct((m, tn), F32), jax.ShapeDtypeStruct((m, tn), F32),
                   jax.ShapeDtypeStruct((m, tn), F32), jax.ShapeDtypeStruct((m, tn), BF16),
                   jax.ShapeDtypeStruct((m, tn), F32), jax.ShapeDtypeStruct((m, 2 * tn), BF16),
                   jax.ShapeDtypeStruct((m, 2 * tn), BF16)],
        scratch_shapes=[pltpu.VMEM((tm, d), BF16)],
        compiler_params=_cparams(("arbitrary", "arbitrary")),
        name="in_proj",
    )(x, sc, sh, g1.reshape(1, d), w_in, b_in.reshape(1, n), g_q.reshape(1, HEAD_DIM), g_k.reshape(1, HEAD_DIM),
      cos, sin, ln_g.reshape(1, tn), ln_b.reshape(1, tn))
    return outs


def _moba_prompt_kernel(q_ref, k_ref, v_ref, o_ref):
    s_len = q_ref.shape[1]
    nb = s_len // MOBA_BLOCK
    blk = MOBA_BLOCK
    q = q_ref[0]
    k = k_ref[0]
    means = jnp.concatenate(
        [jnp.mean(k[n * blk:(n + 1) * blk], axis=0, keepdims=True) for n in range(nb)], axis=0)
    gate = lax.dot_general(means, q, (((1,), (1,)), ((), ())), precision=HIGHEST,
                           preferred_element_type=F32)
    n_iota = lax.broadcasted_iota(I32, (nb, s_len), 0)
    own = lax.broadcasted_iota(I32, (nb, s_len), 1) // blk
    cnt = jnp.zeros((nb, s_len), I32)
    for m in range(nb):
        gm = gate[m:m + 1, :]
        beats = (m < own) & ((gm > gate) | ((gm == gate) & (m < n_iota)))
        cnt = cnt + beats.astype(I32)
    sel = ((n_iota < own) & (cnt < MOBA_TOPK)).astype(F32)
    sel = jnp.concatenate([sel, jnp.zeros((LANES - nb, s_len), F32)], axis=0)

    qb = q.astype(BF16)
    kb = k.astype(BF16)
    vb = v_ref[0].astype(BF16)
    scale = HEAD_DIM ** -0.5
    row = lax.broadcasted_iota(I32, (blk, blk), 0)
    col = lax.broadcasted_iota(I32, (blk, blk), 1)
    for i in range(nb):
        w = (i + 1) * blk
        s = lax.dot_general(qb[i * blk:(i + 1) * blk], kb[:w], (((1,), (1,)), ((), ())),
                            preferred_element_type=F32) * scale
        sel_t = sel[:, i * blk:(i + 1) * blk].T
        parts = [jnp.broadcast_to(sel_t[:, jj:jj + 1], (blk, blk)) for jj in range(i)]
        parts.append(jnp.where(col <= row, 1.0, 0.0))
        allowed = jnp.concatenate(parts, axis=1) if i else parts[0]
        s = jnp.where(allowed > 0.0, s, NEG_BIG)
        mx = jnp.max(s, axis=1, keepdims=True)
        p = jnp.exp(s - mx)
        l = jnp.sum(p, axis=1, keepdims=True)
        o = jnp.dot(p.astype(BF16), vb[:w], preferred_element_type=F32) / l
        o_ref[0, i * blk:(i + 1) * blk, :] = o.astype(o_ref.dtype)


def _moba_prompt(q, k, v):
    b, s, _ = q.shape
    assert s % MOBA_BLOCK == 0
    spec = pl.BlockSpec((1, s, HEAD_DIM), lambda bi, hi: (bi, 0, hi))
    return pl.pallas_call(
        _moba_prompt_kernel,
        grid=(b, N_HEADS),
        in_specs=[spec, spec, spec],
        out_specs=spec,
        out_shape=jax.ShapeDtypeStruct(q.shape, BF16),
        compiler_params=_cparams(("arbitrary", "arbitrary")),
        name="moba_prompt",
    )(q, k, v)


def _moba_sample_kernel(pt_ref, q_ref, kn_ref, vn_ref, kc_hbm, vc_hbm, o_ref,
                        buf, sem, qbd_scr, sc_scr, means_scr, bmax_scr, ebig_scr, selx_scr,
                        pown_scr, l_scr, out_scr, idx_vmem, idx_smem, isem, vbuf, vsem, *, n_pages, n_req):
    pg = PAGES_PER_STEP
    ns = n_pages // pg
    nblk = n_pages // 2
    n_tok = kn_ref.shape[1]
    rows = N_HEADS * n_tok
    page_rows = buf.shape[2]
    page = page_rows // N_HEADS
    r = pl.program_id(0)
    c = pl.program_id(1)
    kstep = r * ns + c
    ktotal = n_req * ns
    slot = kstep % PAGE_SLOTS
    scale = HEAD_DIM ** -0.5

    def fetch(ks_, wait):
        r_ = ks_ // ns
        c_ = ks_ % ns
        sl = ks_ % PAGE_SLOTS
        for p in range(pg):
            pidx = pt_ref[r_ * n_pages + c_ * pg + p]
            src = pl.ds(pl.multiple_of(pidx * page_rows, page_rows), page_rows)
            cp = pltpu.make_async_copy(kc_hbm.at[src, :], buf.at[sl, p], sem.at[sl, p])
            cp.wait() if wait else cp.start()

    def fetch_v(hh, wait):
        vs = hh % 2
        for t in range(n_tok):
            for s in range(MOBA_TOPK):
                blk = idx_smem[hh * n_tok + t, s]
                for half in range(2):
                    pidx = pt_ref[r * n_pages + blk * 2 + half]
                    cp = pltpu.make_async_copy(
                        vc_hbm.at[pl.ds(pl.multiple_of(pidx * page, page), page), hh, :],
                        vbuf.at[vs, t * MOBA_TOPK + s, pl.ds(half * page, page), :], vsem.at[vs])
                    cp.wait() if wait else cp.start()

    def load_page(p):
        return jnp.concatenate([buf[slot, p, pl.ds(hh, LANES, stride=N_HEADS), :] for hh in range(N_HEADS)],
                               axis=1)

    ahead = PAGE_SLOTS - 1

    @pl.when(c < ns)
    def _():
        @pl.when(kstep == 0)
        def _():
            for a in range(ahead):
                fetch(kstep + a, False)
            rb = lax.broadcasted_iota(I32, ebig_scr.shape, 0)
            cb = lax.broadcasted_iota(I32, ebig_scr.shape, 1) // LANES
            ebig_scr[...] = jnp.where(rb == cb, 1.0, 0.0).astype(BF16)

        @pl.when(kstep + ahead < ktotal)
        def _():
            fetch(kstep + ahead, False)

        fetch(kstep, True)

    @pl.when(c == 0)
    def _():
        q8 = q_ref[0]
        qrep = jnp.concatenate([q8] * N_HEADS, axis=0)
        rh = lax.broadcasted_iota(I32, qrep.shape, 0) // n_tok
        ch = lax.broadcasted_iota(I32, qrep.shape, 1) // HEAD_DIM
        qbd_scr[...] = jnp.where(rh == ch, qrep, 0.0)
        bmax_scr[...] = jnp.full(bmax_scr.shape, NEG_BIG, F32)

    @pl.when(c < ns)
    def _():
        qbd = qbd_scr[...].astype(BF16)
        lane = lax.broadcasted_iota(I32, bmax_scr.shape, 1)
        prev_sum = prev_s = None
        for p in range(pg):
            kp = load_page(p)
            s = lax.dot_general(qbd, kp.astype(BF16), (((1,), (1,)), ((), ())), preferred_element_type=F32)
            b_idx = (c * pg + p) // 2
            half = p % 2
            sc_scr[b_idx, :, half * LANES:(half + 1) * LANES] = s
            psum = jnp.sum(kp, axis=0, keepdims=True)
            if half == 0:
                prev_sum, prev_s = psum, s
            else:
                means_scr[pl.ds(b_idx, 1), :] = (prev_sum + psum) * (1.0 / MOBA_BLOCK)
                bm = jnp.max(jnp.maximum(prev_s, s), axis=1, keepdims=True)
                bmax_scr[...] = jnp.where(lane == b_idx, bm, bmax_scr[...])

    @pl.when(c == ns - 1)
    def _():
        qbd = qbd_scr[...]
        gate = lax.dot_general(qbd, means_scr[...], (((1,), (1,)), ((), ())), precision=HIGHEST,
                               preferred_element_type=F32)
        lane = lax.broadcasted_iota(I32, gate.shape, 1).astype(F32)
        pick_lane = lax.broadcasted_iota(I32, idx_vmem.shape, 1)
        g = gate
        sel = jnp.zeros(gate.shape, F32)
        picks = jnp.zeros(idx_vmem.shape, F32)
        for s in range(MOBA_TOPK):
            mx = jnp.max(g, axis=1, keepdims=True)
            idx = jnp.min(jnp.where(g == mx, lane, float(nblk)), axis=1, keepdims=True)
            pick = lane == idx
            sel = jnp.where(pick, 1.0, sel)
            g = jnp.where(pick, -jnp.inf, g)
            picks = jnp.where(pick_lane == s, idx, picks)
        idx_vmem[...] = picks.astype(I32)
        to_smem = pltpu.make_async_copy(idx_vmem, idx_smem, isem.at[0])
        to_smem.start()
        selx_scr[...] = jnp.dot(sel.astype(BF16), ebig_scr[...], preferred_element_type=F32)

        kn = jnp.concatenate([kn_ref[0], jnp.zeros((LANES - n_tok, kn_ref.shape[2]), F32)], axis=0)
        s_own = lax.dot_general(qbd.astype(BF16), kn.astype(BF16), (((1,), (1,)), ((), ())),
                                preferred_element_type=F32) * scale
        t_q = lax.broadcasted_iota(I32, s_own.shape, 0) % n_tok
        t_k = lax.broadcasted_iota(I32, s_own.shape, 1)
        s_own = jnp.where(t_k <= t_q, s_own, NEG_BIG)

        m_sel = jnp.max(jnp.where(sel > 0.0, bmax_scr[:, :nblk] * scale, NEG_BIG), axis=1, keepdims=True)
        m = jnp.maximum(m_sel, jnp.max(s_own, axis=1, keepdims=True))
        m_b = jnp.broadcast_to(m, (rows, LANES))

        def exp_body(b_idx, lacc):
            off = pl.multiple_of(b_idx * LANES, LANES)
            msk = selx_scr[:, pl.ds(off, LANES)] > 0.0
            s = sc_scr[b_idx]
            p0 = jnp.where(msk, jnp.exp(s[:, :LANES] * scale - m_b), 0.0)
            p1 = jnp.where(msk, jnp.exp(s[:, LANES:] * scale - m_b), 0.0)
            sc_scr[b_idx, :, :LANES] = p0
            sc_scr[b_idx, :, LANES:] = p1
            return lacc + p0 + p1

        p_own = jnp.exp(s_own - m_b)
        lacc = lax.fori_loop(0, nblk, exp_body, p_own, unroll=4)
        pown_scr[...] = p_own
        l_scr[...] = jnp.broadcast_to(jnp.sum(lacc, axis=1, keepdims=True), l_scr.shape)
        to_smem.wait()
        fetch_v(0, False)

    @pl.when(c >= ns)
    def _():
        hh = c - ns
        vs = hh % 2

        @pl.when(hh + 1 < N_HEADS)
        def _():
            fetch_v(hh + 1, False)

        fetch_v(hh, True)
        row0 = pl.multiple_of(hh * n_tok, n_tok)
        col0 = pl.multiple_of(hh * HEAD_DIM, HEAD_DIM)
        vn = jnp.concatenate([vn_ref[0, :, pl.ds(col0, HEAD_DIM)], jnp.zeros((LANES - n_tok, HEAD_DIM), F32)], axis=0)
        acc = jnp.dot(pown_scr[pl.ds(row0, n_tok), :].astype(BF16), vn.astype(BF16), preferred_element_type=F32)
        tok = lax.broadcasted_iota(I32, (n_tok, MOBA_BLOCK), 0)
        for t in range(n_tok):
            for s in range(MOBA_TOPK):
                blk = idx_smem[hh * n_tok + t, s]
                w = jnp.where(tok == t, sc_scr[blk, pl.ds(row0, n_tok), :], 0.0).astype(BF16)
                acc = acc + jnp.dot(w, vbuf[vs, t * MOBA_TOPK + s].astype(BF16), preferred_element_type=F32)
        out_scr[:, pl.ds(col0, HEAD_DIM)] = acc / l_scr[pl.ds(row0, n_tok), 0:1]

    @pl.when(c == ns + N_HEADS - 1)
    def _():
        o_ref[0] = out_scr[...].astype(o_ref.dtype)


def _moba_sample(q, k_new, v_new, cache_k, cache_v, page_table, page):
    n_req, n_tok, width = q.shape
    n_pages = page_table.shape[1]
    assert page == LANES and 2 * page == MOBA_BLOCK and n_pages % (2 * PAGES_PER_STEP) == 0
    assert n_tok % 8 == 0 and n_tok <= LANES and n_pages // 2 <= LANES
    assert n_req * (n_pages // PAGES_PER_STEP) >= PAGE_SLOTS and n_pages // 2 >= MOBA_TOPK
    ns = n_pages // PAGES_PER_STEP
    nblk = n_pages // 2
    rows = N_HEADS * n_tok
    tok_spec = pl.BlockSpec((1, n_tok, width), lambda r, c, pt: (r, 0, 0))
    kern = functools.partial(_moba_sample_kernel, n_pages=n_pages, n_req=n_req)
    return pl.pallas_call(
        kern,
        grid_spec=pltpu.PrefetchScalarGridSpec(
            num_scalar_prefetch=1,
            grid=(n_req, ns + N_HEADS),
            in_specs=[tok_spec, tok_spec, tok_spec,
                      pl.BlockSpec(memory_space=pl.ANY), pl.BlockSpec(memory_space=pl.ANY)],
            out_specs=tok_spec,
            scratch_shapes=[pltpu.VMEM((PAGE_SLOTS, PAGES_PER_STEP, page * N_HEADS, HEAD_DIM), F32),
                            pltpu.SemaphoreType.DMA((PAGE_SLOTS, PAGES_PER_STEP)),
                            pltpu.VMEM((rows, width), F32),
                            pltpu.VMEM((nblk, rows, MOBA_BLOCK), F32),
                            pltpu.VMEM((nblk, width), F32),
                            pltpu.VMEM((rows, LANES), F32),
                            pltpu.VMEM((nblk, nblk * LANES), BF16),
                            pltpu.VMEM((rows, nblk * LANES), F32),
                            pltpu.VMEM((rows, LANES), F32),
                            pltpu.VMEM((rows, LANES), F32),
                            pltpu.VMEM((n_tok, width), F32),
                            pltpu.VMEM((rows, LANES), I32),
                            pltpu.SMEM((rows, LANES), I32),
                            pltpu.SemaphoreType.DMA((1,)),
                            pltpu.VMEM((2, n_tok * MOBA_TOPK, MOBA_BLOCK, HEAD_DIM), F32),
                            pltpu.SemaphoreType.DMA((2,))]),
        out_shape=jax.ShapeDtypeStruct(q.shape, BF16),
        compiler_params=_cparams(("arbitrary", "arbitrary")),
        name="moba_sample",
    )(page_table.reshape(-1), q, k_new, v_new, cache_k, cache_v)


def _mix_out_kernel(u_ref, gv_ref, sga_ref, sgb_ref, yb_ref, x_ref, gt1_ref, sc2_ref, sh2_ref, g2_ref,
                    ws_ref, bs_ref, wa_ref, wb_ref, wo_ref, wr_ref, br_ref,
                    x1_ref, h2_ref, idx_ref, gate_ref, ya_scr):
    tm = u_ref.shape[0]
    cs = ws_ref.shape[1]
    row = lax.broadcasted_iota(I32, (cs, cs), 0)
    col = lax.broadcasted_iota(I32, (cs, cs), 1)
    causal = col <= row
    bs = bs_ref[...]
    for g in range(GMLP_GROUPS):
        ws = jnp.where(causal, ws_ref[g], 0.0).astype(BF16)
        gs = slice(g * GMLP_GROUP_DIM, (g + 1) * GMLP_GROUP_DIM)
        for ci in range(tm // cs):
            rs = slice(ci * cs, (ci + 1) * cs)
            mixed = jnp.dot(ws, gv_ref[rs, gs].astype(BF16), preferred_element_type=F32) + bs[:, g:g + 1]
            ya_scr[rs, gs] = (u_ref[rs, gs].astype(F32) * mixed).astype(BF16)

    a = jnp.dot(ya_scr[...], wa_ref[...], preferred_element_type=F32)
    b = jnp.dot(yb_ref[...], wb_ref[...], preferred_element_type=F32)
    merged = sga_ref[...].astype(F32) * a + sgb_ref[...].astype(F32) * b
    mix = jnp.dot(merged.astype(BF16), wo_ref[...], preferred_element_type=F32)
    x1 = x_ref[...] + gt1_ref[0] * mix
    x1_ref[...] = x1
    r = lax.rsqrt(jnp.mean(x1 * x1, axis=-1, keepdims=True) + EPS)
    h2 = (x1 * r * g2_ref[...]) * (1.0 + sc2_ref[0]) + sh2_ref[0]
    h2_ref[...] = h2

    h_hi = h2.astype(BF16)
    h_lo = (h2 - h_hi.astype(F32)).astype(BF16)
    w_hi = wr_ref[0]
    logits = (jnp.dot(h_hi, w_hi, preferred_element_type=F32) + jnp.dot(h_lo, w_hi, preferred_element_type=F32)
              + jnp.dot(h_hi, wr_ref[1], preferred_element_type=F32)) + br_ref[...]
    lane = lax.broadcasted_iota(I32, logits.shape, 1)
    lane_f = lane.astype(F32)
    g = jnp.where(lane < N_EXPERTS, logits, -jnp.inf)
    vals, idxs = [], []
    for _ in range(TOP_K):
        mx = jnp.max(g, axis=1, keepdims=True)
        idx = jnp.min(jnp.where(g == mx, lane_f, float(LANES)), axis=1, keepdims=True)
        vals.append(mx)
        idxs.append(idx.astype(I32))
        g = jnp.where(lane_f == idx, -jnp.inf, g)
    es = [jnp.exp(v - vals[0]) for v in vals]
    den = es[0] + es[1] + es[2] + es[3]
    idx_out = jnp.zeros(logits.shape, I32)
    gate_out = jnp.zeros(logits.shape, F32)
    for kk in range(TOP_K):
        idx_out = jnp.where(lane == kk, idxs[kk], idx_out)
        gate_out = jnp.where(lane == kk, es[kk] / den, gate_out)
    idx_ref[...] = idx_out
    gate_ref[...] = gate_out


def _mix_out(u, gv, sga, sgb, yb, x, gt1, sc2, sh2, g2, ws, bs_t, w_a, w_b, w_o, w_r, b_r, tm):
    m, d = x.shape
    nb, r, _ = gt1.shape
    tiles_per_mod = m // (nb * tm)
    cs = ws.shape[1]
    assert m % tm == 0 and tm % cs == 0
    gw = u.shape[1]
    row = lambda i: (i, 0)
    const2 = lambda i: (0, 0)
    mod_spec = pl.BlockSpec((1, r, d), lambda i: (i // tiles_per_mod, 0, 0))
    resident = dict(pipeline_mode=pl.Buffered(1))
    return pl.pallas_call(
        _mix_out_kernel,
        grid=(m // tm,),
        in_specs=[pl.BlockSpec((tm, gw), row), pl.BlockSpec((tm, gw), row),
                  pl.BlockSpec((tm, d), row), pl.BlockSpec((tm, d), row),
                  pl.BlockSpec((tm, ATTN_WIDTH), row), pl.BlockSpec((tm, d), row),
                  mod_spec, mod_spec, mod_spec,
                  pl.BlockSpec((1, d), const2),
                  pl.BlockSpec(ws.shape, lambda i: (0, 0, 0), **resident),
                  pl.BlockSpec(bs_t.shape, const2),
                  pl.BlockSpec(w_a.shape, const2, **resident),
                  pl.BlockSpec(w_b.shape, const2, **resident),
                  pl.BlockSpec(w_o.shape, const2, **resident),
                  pl.BlockSpec(w_r.shape, lambda i: (0, 0, 0), **resident),
                  pl.BlockSpec((1, LANES), const2)],
        out_specs=[pl.BlockSpec((tm, d), row), pl.BlockSpec((tm, d), row),
                   pl.BlockSpec((tm, LANES), row), pl.BlockSpec((tm, LANES), row)],
        out_shape=[jax.ShapeDtypeStruct((m, d), F32), jax.ShapeDtypeStruct((m, d), F32),
                   jax.ShapeDtypeStruct((m, LANES), I32), jax.ShapeDtypeStruct((m, LANES), F32)],
        scratch_shapes=[pltpu.VMEM((tm, gw), BF16)],
        compiler_params=_cparams(("arbitrary",)),
        name="mix_out",
    )(u, gv, sga, sgb, yb, x, gt1, sc2, sh2, g2.reshape(1, d), ws, bs_t, w_a, w_b, w_o, w_r, b_r)


def _gather_kernel(src_ref, nused_ref, h_hbm, o_ref, buf, sem):
    rb = pl.program_id(0)
    tb = o_ref.shape[0]
    slot = rb % 2
    nused = nused_ref[0]

    def row_copy(blk, sl, i):
        tok = src_ref[blk * tb + i]
        return pltpu.make_async_copy(h_hbm.at[pl.ds(tok, 1), :], buf.at[sl, pl.ds(i, 1), :], sem.at[sl])

    def start_block(blk, sl):
        def body(i, carry):
            row_copy(blk, sl, i).start()
            return carry
        lax.fori_loop(0, tb, body, 0, unroll=8)

    @pl.when(rb == 0)
    def _():
        start_block(rb, slot)

    @pl.when(rb + 1 < nused)
    def _():
        start_block(rb + 1, 1 - slot)

    @pl.when(rb < nused)
    def _():
        pltpu.make_async_copy(h_hbm.at[pl.ds(0, tb), :], buf.at[slot], sem.at[slot]).wait()
        o_ref[...] = buf[slot].astype(o_ref.dtype)

    @pl.when(rb >= nused)
    def _():
        o_ref[...] = jnp.zeros(o_ref.shape, o_ref.dtype)


def _gather_rows(h2, src_tok, n_used, n_blocks):
    d = h2.shape[1]
    return pl.pallas_call(
        _gather_kernel,
        grid_spec=pltpu.PrefetchScalarGridSpec(
            num_scalar_prefetch=2,
            grid=(n_blocks,),
            in_specs=[pl.BlockSpec(memory_space=pl.ANY)],
            out_specs=pl.BlockSpec((MOE_ROWS, d), lambda rb, src, nu: (rb, 0)),
            scratch_shapes=[pltpu.VMEM((2, MOE_ROWS, d), F32), pltpu.SemaphoreType.DMA((2,))]),
        out_shape=jax.ShapeDtypeStruct((n_blocks * MOE_ROWS, d), BF16),
        compiler_params=_cparams(("arbitrary",)),
        name="moe_gather",
    )(src_tok, n_used, h2)


def _stream_weights(brun_ref, rune_ref, meta_ref, w_hbm, wbuf, sem, w_scrs, part_bases, width):
    n = pl.program_id(0)
    rb = pl.program_id(1)
    n_runs = meta_ref[1]
    run = brun_ref[rb]
    g = n * n_runs + run
    first = (rb == 0) | (brun_ref[jnp.maximum(rb - 1, 0)] != run)

    def copies(gg):
        e = rune_ref[gg % n_runs]
        nn = gg // n_runs
        sl = gg % 2
        return [pltpu.make_async_copy(
            w_hbm.at[e, :, pl.ds(pl.multiple_of(base + nn * width, width), width)], wbuf.at[sl, part], sem.at[sl, part])
            for part, base in enumerate(part_bases)]

    @pl.when(first)
    def _():
        @pl.when(g == 0)
        def _():
            for cp in copies(g):
                cp.start()

        @pl.when(g + 1 < pl.num_programs(0) * n_runs)
        def _():
            for cp in copies(g + 1):
                cp.start()

        for cp in copies(g):
            cp.wait()
        for part, scr in enumerate(w_scrs):
            scr[...] = wbuf[g % 2, part].astype(BF16)


def _up_kernel(be_ref, brun_ref, rune_ref, meta_ref, x_ref, bg_ref, bl_ref, w_hbm, o_ref,
               wbuf, sem, wg_scr, wl_scr, *, de):
    rb = pl.program_id(1)
    n_used = meta_ref[0]
    _stream_weights(brun_ref, rune_ref, meta_ref, w_hbm, wbuf, sem, (wg_scr, wl_scr), (0, de), wg_scr.shape[1])

    @pl.when(rb < n_used)
    def _():
        x = x_ref[...]
        glu = jnp.dot(x, wg_scr[...], preferred_element_type=F32) + bg_ref[0]
        lin = jnp.dot(x, wl_scr[...], preferred_element_type=F32) + bl_ref[0]
        glu = jnp.minimum(glu, SWIGLU_LIMIT)
        lin = jnp.clip(lin, -SWIGLU_LIMIT, SWIGLU_LIMIT)
        o_ref[...] = (glu * jax.nn.sigmoid(SWIGLU_ALPHA * glu) * (lin + 1.0)).astype(o_ref.dtype)

    @pl.when(rb >= n_used)
    def _():
        o_ref[...] = jnp.zeros(o_ref.shape, o_ref.dtype)


def _expert_up(xs, w1, b1, tables, hc):
    p, d = xs.shape
    de = w1.shape[2] // 2
    n_blocks = p // MOE_ROWS
    nh = de // hc
    b1r = b1.reshape(N_EXPERTS, 1, 2 * de)
    return pl.pallas_call(
        functools.partial(_up_kernel, de=de),
        grid_spec=pltpu.PrefetchScalarGridSpec(
            num_scalar_prefetch=4,
            grid=(nh, n_blocks),
            in_specs=[pl.BlockSpec((MOE_ROWS, d), lambda n, rb, be, br, re, mt: (jnp.minimum(rb, mt[0] - 1), 0)),
                      pl.BlockSpec((1, 1, hc), lambda n, rb, be, br, re, mt: (be[rb], 0, n)),
                      pl.BlockSpec((1, 1, hc), lambda n, rb, be, br, re, mt: (be[rb], 0, nh + n)),
                      pl.BlockSpec(memory_space=pl.ANY)],
            out_specs=pl.BlockSpec((MOE_ROWS, hc), lambda n, rb, be, br, re, mt: (rb, n)),
            scratch_shapes=[pltpu.VMEM((2, 2, d, hc), F32), pltpu.SemaphoreType.DMA((2, 2)),
                            pltpu.VMEM((d, hc), BF16), pltpu.VMEM((d, hc), BF16)]),
        out_shape=jax.ShapeDtypeStruct((p, de), BF16),
        compiler_params=_cparams(("arbitrary", "arbitrary")),
        name="expert_up",
    )(*tables, xs, b1r, b1r, w1)


def _down_kernel(be_ref, brun_ref, rune_ref, meta_ref, a_ref, b_ref, w_hbm, o_ref, wbuf, sem, w_scr):
    rb = pl.program_id(1)
    n_used = meta_ref[0]
    _stream_weights(brun_ref, rune_ref, meta_ref, w_hbm, wbuf, sem, (w_scr,), (0,), w_scr.shape[1])

    @pl.when(rb < n_used)
    def _():
        o_ref[...] = jnp.dot(a_ref[...], w_scr[...], preferred_element_type=F32) + b_ref[0]

    @pl.when(rb >= n_used)
    def _():
        o_ref[...] = jnp.zeros(o_ref.shape, o_ref.dtype)


def _expert_down(act, w2, b2, tables, oc):
    p, de = act.shape
    d = w2.shape[2]
    n_blocks = p // MOE_ROWS
    b2r = b2.reshape(N_EXPERTS, 1, d)
    return pl.pallas_call(
        _down_kernel,
        grid_spec=pltpu.PrefetchScalarGridSpec(
            num_scalar_prefetch=4,
            grid=(d // oc, n_blocks),
            in_specs=[pl.BlockSpec((MOE_ROWS, de), lambda n, rb, be, br, re, mt: (jnp.minimum(rb, mt[0] - 1), 0)),
                      pl.BlockSpec((1, 1, oc), lambda n, rb, be, br, re, mt: (be[rb], 0, n)),
                      pl.BlockSpec(memory_space=pl.ANY)],
            out_specs=pl.BlockSpec((MOE_ROWS, oc), lambda n, rb, be, br, re, mt: (rb, n)),
            scratch_shapes=[pltpu.VMEM((2, 1, de, oc), F32), pltpu.SemaphoreType.DMA((2, 1)),
                            pltpu.VMEM((de, oc), BF16)]),
        out_shape=jax.ShapeDtypeStruct((p, d), F32),
        compiler_params=_cparams(("arbitrary", "arbitrary")),
        name="expert_down",
    )(*tables, act, b2r, w2)


def _combine_kernel(dest_ref, y_hbm, x1_ref, gate_ref, gt2_ref, o_ref, buf, sem, *, tok0):
    i = pl.program_id(0)
    n = pl.num_programs(0)
    tm = o_ref.shape[0]
    slot = i % 2

    def row_copy(tile, sl, t, kk):
        src = dest_ref[(tok0 + tile * tm + t) * TOP_K + kk]
        return pltpu.make_async_copy(y_hbm.at[pl.ds(src, 1), :], buf.at[sl, kk, pl.ds(t, 1), :], sem.at[sl])

    def start_rows(tile, sl):
        def body(t, carry):
            for kk in range(TOP_K):
                row_copy(tile, sl, t, kk).start()
            return carry
        lax.fori_loop(0, tm, body, 0, unroll=2)

    @pl.when(i == 0)
    def _():
        start_rows(i, slot)

    @pl.when(i + 1 < n)
    def _():
        start_rows(i + 1, 1 - slot)

    for kk in range(TOP_K):
        pltpu.make_async_copy(y_hbm.at[pl.ds(0, tm), :], buf.at[slot, kk], sem.at[slot]).wait()
    gates = gate_ref[...]
    ff = gates[:, 0:1] * buf[slot, 0]
    for kk in range(1, TOP_K):
        ff = ff + gates[:, kk:kk + 1] * buf[slot, kk]
    o_ref[...] = x1_ref[...] + gt2_ref[0] * ff


def _combine(y_sorted, dest, x1, gates, gt2, tok0, tm):
    m, d = x1.shape
    nb, r, _ = gt2.shape
    tiles_per_mod = m // (nb * tm)
    kern = functools.partial(_combine_kernel, tok0=tok0)
    return pl.pallas_call(
        kern,
        grid_spec=pltpu.PrefetchScalarGridSpec(
            num_scalar_prefetch=1,
            grid=(m // tm,),
            in_specs=[pl.BlockSpec(memory_space=pl.ANY),
                      pl.BlockSpec((tm, d), lambda i, de: (i, 0)),
                      pl.BlockSpec((tm, LANES), lambda i, de: (i, 0)),
                      pl.BlockSpec((1, r, d), lambda i, de: (i // tiles_per_mod, 0, 0))],
            out_specs=pl.BlockSpec((tm, d), lambda i, de: (i, 0)),
            scratch_shapes=[pltpu.VMEM((2, TOP_K, tm, d), F32), pltpu.SemaphoreType.DMA((2,))]),
        out_shape=jax.ShapeDtypeStruct((m, d), F32),
        compiler_params=_cparams(("arbitrary",)),
        name="moe_combine",
    )(dest, y_sorted, x1, gates, gt2)


def _rope_tables(pos):
    half = HEAD_DIM // 2
    inv_freq = ROPE_THETA ** (-jnp.arange(half, dtype=F32) / half)
    ang = pos.astype(F32)[:, None] * inv_freq[None, :]
    cos = jnp.cos(ang)
    sin = jnp.sin(ang)
    return jnp.concatenate([cos, cos], axis=1), jnp.concatenate([-sin, sin], axis=1)


def _routing_tables(top_idx, n_blocks):
    e_flat = top_idx.reshape(-1)
    a = e_flat.shape[0]
    ck = LANES
    assert a % ck == 0
    onehot = e_flat[:, None] == jnp.arange(N_EXPERTS, dtype=I32)[None, :]
    within = jnp.einsum('ts,csn->ctn', jnp.tril(jnp.ones((ck, ck), BF16)), onehot.astype(BF16).reshape(a // ck, ck, -1),
                        preferred_element_type=F32)
    tot = within[:, -1, :]
    before = jnp.dot(jnp.tril(jnp.ones((a // ck, a // ck), F32), -1), tot, precision=HIGHEST)
    csum = (within + before[:, None, :]).reshape(a, N_EXPERTS).astype(I32)
    counts = csum[-1]
    rank = jnp.sum(jnp.where(onehot, csum, 0), axis=1) - 1
    padded = (counts + MOE_ROWS - 1) // MOE_ROWS * MOE_ROWS
    pad_end = jnp.cumsum(padded)
    pad_start = pad_end - padded
    dest = pad_start[e_flat] + rank
    src_tok = jnp.zeros((n_blocks * MOE_ROWS,), I32).at[dest].set(jnp.arange(a, dtype=I32) // TOP_K)
    n_used = pad_end[-1] // MOE_ROWS
    blk = jnp.arange(n_blocks, dtype=I32)
    block_e = jnp.minimum(jnp.sum((pad_end[None, :] <= blk[:, None] * MOE_ROWS).astype(I32), axis=1), N_EXPERTS - 1)
    block_e = jnp.where(blk < n_used, block_e, block_e[n_used - 1])
    first = jnp.concatenate([jnp.ones((1,), bool), block_e[1:] != block_e[:-1]])
    block_run = jnp.cumsum(first.astype(I32)) - 1
    run_e = jnp.sum(jnp.where((block_run[None, :] == jnp.arange(N_EXPERTS, dtype=I32)[:, None]) & first[None, :],
                              block_e[None, :], 0), axis=1).astype(I32)
    meta = jnp.stack([n_used, block_run[-1] + 1]).astype(I32)
    return dest.astype(I32), src_tok, (block_e, block_run.astype(I32), run_e, meta)


def kernel(x_prompt, x_sample, cache_k, cache_v, page_table, c_prompt, c_sample, w_ada, b_ada, g_norm1, g_norm2, w_in, b_in, g_q, g_k, gmlp_ln_g, gmlp_ln_b, w_spatial, b_spatial, w_branch_a, w_branch_b, w_out, w_router, b_router, w_expert_in, b_expert_in, w_expert_out, b_expert_out):
    depth = w_ada.shape[0]
    assert depth == 1
    bp, sp, d = x_prompt.shape
    bs_, ss, _ = x_sample.shape
    n_pages = page_table.shape[1]
    page = cache_k.shape[2]
    past_len = n_pages * page
    mp, ms = bp * sp, bs_ * ss
    tm_p, tm_s = 512, ms

    c_all = jnp.concatenate([c_prompt, c_sample], axis=0)
    pad = (-c_all.shape[0]) % 16
    c_all = jnp.pad(c_all, ((0, pad), (0, 0)))
    mod = _adaln(c_all, w_ada[0], b_ada[0])
    mod_p = [mod[:bp, i * d:(i + 1) * d].reshape(bp, 1, d) for i in range(6)]
    mod_s = [jnp.broadcast_to(mod[bp:bp + bs_, None, i * d:(i + 1) * d], (bs_, ss, d)).reshape(1, ms, d)
             for i in range(6)]

    w_in_b = w_in[0].astype(BF16)
    cos_p, sin_p = _rope_tables(jnp.arange(sp, dtype=I32))
    cos_s, sin_s = _rope_tables(past_len + jnp.arange(ss, dtype=I32))
    cos_s, sin_s = jnp.tile(cos_s, (bs_, 1)), jnp.tile(sin_s, (bs_, 1))

    xp = x_prompt.reshape(mp, d)
    xs = x_sample.reshape(ms, d)
    proj = functools.partial(_in_proj, g1=g_norm1[0], w_in=w_in_b, b_in=b_in[0], g_q=g_q[0], g_k=g_k[0],
                             ln_g=gmlp_ln_g[0], ln_b=gmlp_ln_b[0])
    qp, kp, vp, up, gvp, sgap, sgbp = proj(xp, mod_p[1], mod_p[0], cos=cos_p, sin=sin_p, tm=tm_p)
    qs, ks, vs, us, gvs, sgas, sgbs = proj(xs, mod_s[1], mod_s[0], cos=cos_s, sin=sin_s, tm=tm_s)

    yb_p = _moba_prompt(qp.reshape(bp, sp, ATTN_WIDTH), kp.reshape(bp, sp, ATTN_WIDTH),
                        vp.reshape(bp, sp, ATTN_WIDTH)).reshape(mp, ATTN_WIDTH)
    yb_s = _moba_sample(qs.reshape(bs_, ss, ATTN_WIDTH), ks.reshape(bs_, ss, ATTN_WIDTH),
                        vs.reshape(bs_, ss, ATTN_WIDTH), cache_k.reshape(-1, HEAD_DIM),
                        cache_v.reshape(-1, N_HEADS, HEAD_DIM), page_table, page).reshape(ms, ATTN_WIDTH)

    ws_p = w_spatial[0]
    bs_p = jnp.transpose(b_spatial[0])
    tile_t = (jnp.arange(ms, dtype=I32)[:, None] % ss == jnp.arange(ss, dtype=I32)[None, :]).astype(F32)
    same_req = jnp.arange(ms, dtype=I32)[:, None] // ss == jnp.arange(ms, dtype=I32)[None, :] // ss
    ws_s = jnp.where(same_req[None], jnp.einsum('it,gts,js->gij', tile_t, w_spatial[0][:, :ss, :ss], tile_t,
                                                precision=HIGHEST), 0.0)
    bs_s = jnp.tile(bs_p[:ss], (bs_, 1))
    w_a_b, w_b_b, w_o_b = w_branch_a[0].astype(BF16), w_branch_b[0].astype(BF16), w_out[0].astype(BF16)
    w_r_pad = jnp.pad(w_router[0], ((0, 0), (0, LANES - N_EXPERTS)))
    w_r_hi = w_r_pad.astype(BF16)
    w_r_pad = jnp.stack([w_r_hi, (w_r_pad - w_r_hi.astype(F32)).astype(BF16)])
    b_r_pad = jnp.pad(b_router[0], (0, LANES - N_EXPERTS)).reshape(1, LANES)
    mix = functools.partial(_mix_out, g2=g_norm2[0], w_a=w_a_b, w_b=w_b_b, w_o=w_o_b, w_r=w_r_pad, b_r=b_r_pad)
    n_tok = mp + ms
    x1p, h2p, idxp, gatep = mix(up, gvp, sgap, sgbp, yb_p, xp, mod_p[2], mod_p[4], mod_p[3],
                                ws=ws_p, bs_t=bs_p, tm=256)
    x1s, h2s, idxs, gates = mix(us, gvs, sgas, sgbs, yb_s, xs, mod_s[2], mod_s[4], mod_s[3],
                                ws=ws_s, bs_t=bs_s, tm=ms)

    h2 = jnp.concatenate([h2p, h2s], axis=0)
    top_idx = jnp.concatenate([idxp[:, :TOP_K], idxs[:, :TOP_K]], axis=0)
    n_assign = n_tok * TOP_K
    n_blocks = (n_assign + N_EXPERTS * (MOE_ROWS - 1) + MOE_ROWS - 1) // MOE_ROWS
    dest, src_tok, tables = _routing_tables(top_idx, n_blocks)
    x_sorted = _gather_rows(h2, src_tok, tables[3][:1], n_blocks)
    act = _expert_up(x_sorted, w_expert_in[0], b_expert_in[0], tables, hc=1024)
    y_sorted = _expert_down(act, w_expert_out[0], b_expert_out[0], tables, oc=2048)
    yp = _combine(y_sorted, dest, x1p, gatep, mod_p[5], tok0=0, tm=256)
    ys = _combine(y_sorted, dest, x1s, gates, mod_s[5], tok0=mp, tm=ms)

    shp = (depth, bp, sp, N_HEADS, HEAD_DIM)
    shs = (depth, bs_, ss, N_HEADS, HEAD_DIM)
    return (yp.reshape(bp, sp, d), ys.reshape(bs_, ss, d), kp.reshape(shp), vp.reshape(shp),
            ks.reshape(shs), vs.reshape(shs), gvs.reshape(depth, bs_, ss, gvs.shape[1]))
```

```python
import functools
import math

import jax
import jax.numpy as jnp
from jax import lax
from jax.experimental import pallas as pl
from jax.experimental.pallas import tpu as pltpu

F32 = jnp.float32
BF16 = jnp.bfloat16
I32 = jnp.int32
HIGHEST = lax.Precision.HIGHEST

N_HEADS = 8
HEAD_DIM = 128
ATTN_WIDTH = N_HEADS * HEAD_DIM
MOBA_BLOCK = 256
MOBA_TOPK = 3
ROPE_THETA = 10000.0
GMLP_GROUPS = 8
GMLP_GROUP_DIM = 128
CHUNK = 128
N_EXPERTS = 32
TOP_K = 4
SWIGLU_ALPHA = 1.702
SWIGLU_LIMIT = 7.0
EPS = 1e-6
NEG_BIG = -1e30

LANES = 128
MXU_COLS = 256
MOE_ROWS = 256
PAGES_PER_STEP = 16
PAGE_SLOTS = 3
VBUF_SLOTS = 3
VMEM_LIMIT = 56 * 1024 * 1024


def _cparams(sem):
    return pltpu.CompilerParams(dimension_semantics=sem, vmem_limit_bytes=VMEM_LIMIT)


def _ada_kernel(c_ref, w_ref, b_ref, o_ref):
    c = c_ref[...]
    s = (c * jax.nn.sigmoid(c)).astype(BF16)
    o_ref[...] = jnp.dot(s, w_ref[...].astype(BF16), preferred_element_type=F32) + b_ref[...]


def _adaln(c, w_ada, b_ada):
    m, d = c.shape
    n = w_ada.shape[1]
    tn = 1024
    return pl.pallas_call(
        _ada_kernel,
        grid=(n // tn,),
        in_specs=[pl.BlockSpec((m, d), lambda j: (0, 0)),
                  pl.BlockSpec((d, tn), lambda j: (0, j)),
                  pl.BlockSpec((1, tn), lambda j: (0, j))],
        out_specs=pl.BlockSpec((m, tn), lambda j: (0, j)),
        out_shape=jax.ShapeDtypeStruct((m, n), F32),
        compiler_params=_cparams(("arbitrary",)),
        name="adaln",
    )(c, w_ada, b_ada.reshape(1, n))


def _gelu(x):
    return 0.5 * x * (1.0 + lax.erf(x * (1.0 / math.sqrt(2.0))))


def _in_kernel(x_ref, sc_ref, sh_ref, g1_ref, w_ref, b_ref, gq_ref, gk_ref, cos_ref, sin_ref, lng_ref, lnb_ref,
               q_ref, k_ref, v_ref, u_ref, gv_ref, sga_ref, sgb_ref, h_scr):
    j = pl.program_id(1)

    @pl.when(j == 0)
    def _():
        x = x_ref[...]
        r = lax.rsqrt(jnp.mean(x * x, axis=-1, keepdims=True) + EPS)
        h = (x * r * g1_ref[...]) * (1.0 + sc_ref[0]) + sh_ref[0]
        h_scr[...] = h.astype(BF16)

    tn = w_ref.shape[1]
    chunks = [slice(c0, c0 + MXU_COLS) for c0 in range(0, tn, MXU_COLS)]

    def tile(cs):
        return jnp.dot(h_scr[...], w_ref[:, cs], preferred_element_type=F32) + b_ref[:, cs]

    def qk_post(g_ref, o_ref):
        cos = cos_ref[...]
        sin = sin_ref[...]
        g = g_ref[...]
        for cs in chunks:
            z = tile(cs)
            for hh in range(MXU_COLS // HEAD_DIM):
                zh = z[:, hh * HEAD_DIM:(hh + 1) * HEAD_DIM]
                r = lax.rsqrt(jnp.mean(zh * zh, axis=-1, keepdims=True) + EPS)
                y = zh * r * g
                c0 = cs.start + hh * HEAD_DIM
                o_ref[:, c0:c0 + HEAD_DIM] = y * cos + pltpu.roll(y, HEAD_DIM // 2, 1) * sin

    @pl.when(j == 0)
    def _():
        qk_post(gq_ref, q_ref)

    @pl.when(j == 1)
    def _():
        qk_post(gk_ref, k_ref)

    @pl.when(j == 2)
    def _():
        for cs in chunks:
            v_ref[:, cs] = tile(cs)

    @pl.when(j == 3)
    def _():
        for cs in chunks:
            u_ref[:, cs] = _gelu(tile(cs)).astype(u_ref.dtype)

    @pl.when(j == 4)
    def _():
        total = None
        for cs in chunks:
            a = _gelu(tile(cs))
            gv_ref[:, cs] = a
            part = jnp.sum(a, axis=-1, keepdims=True)
            total = part if total is None else total + part
        mu = total * (1.0 / tn)
        d = gv_ref[...] - mu
        var = jnp.mean(d * d, axis=-1, keepdims=True)
        gv_ref[...] = d * lax.rsqrt(var + EPS) * lng_ref[...] + lnb_ref[...]

    @pl.when((j == 5) | (j == 6))
    def _():
        for cs in chunks:
            sga_ref[:, cs] = jax.nn.sigmoid(tile(cs)).astype(sga_ref.dtype)

    @pl.when(j >= 7)
    def _():
        for cs in chunks:
            sgb_ref[:, cs] = jax.nn.sigmoid(tile(cs)).astype(sgb_ref.dtype)


def _in_proj(x, sc, sh, g1, w_in, b_in, g_q, g_k, cos, sin, ln_g, ln_b, tm):
    m, d = x.shape
    n = w_in.shape[1]
    tn = 1024
    assert n == 9 * tn and m % tm == 0
    nb, r, _ = sc.shape
    tiles_per_mod = m // (nb * tm)
    tiles_per_tab = cos.shape[0] // tm
    row = lambda i, j: (i, 0)
    mod_spec = pl.BlockSpec((1, r, d), lambda i, j: (i // tiles_per_mod, 0, 0))
    vec_d = pl.BlockSpec((1, d), lambda i, j: (0, 0))
    vec_h = pl.BlockSpec((1, HEAD_DIM), lambda i, j: (0, 0))
    vec_n = pl.BlockSpec((1, tn), lambda i, j: (0, 0))
    tab = pl.BlockSpec((tm, HEAD_DIM), lambda i, j: (i % tiles_per_tab, 0))
    outs = pl.pallas_call(
        _in_kernel,
        grid=(m // tm, n // tn),
        in_specs=[pl.BlockSpec((tm, d), row), mod_spec, mod_spec, vec_d,
                  pl.BlockSpec((d, tn), lambda i, j: (0, j)),
                  pl.BlockSpec((1, tn), lambda i, j: (0, j)),
                  vec_h, vec_h, tab, tab, vec_n, vec_n],
        out_specs=[pl.BlockSpec((tm, tn), row), pl.BlockSpec((tm, tn), row), pl.BlockSpec((tm, tn), row),
                   pl.BlockSpec((tm, tn), row), pl.BlockSpec((tm, tn), row),
                   pl.BlockSpec((tm, tn), lambda i, j: (i, jnp.clip(j - 5, 0, 1))),
                   pl.BlockSpec((tm, tn), lambda i, j: (i, jnp.clip(j - 7, 0, 1)))],
        out_shape=[jax.ShapeDtypeStruct((m, tn), F32), jax.ShapeDtypeStruct((m, tn), F32),
                   jax.ShapeDtypeStruct((m, tn), F32), jax.ShapeDtypeStruct((m, tn), BF16),
                   jax.ShapeDtypeStruct((m, tn), F32), jax.ShapeDtypeStruct((m, 2 * tn), BF16),
                   jax.ShapeDtypeStruct((m, 2 * tn), BF16)],
        scratch_shapes=[pltpu.VMEM((tm, d), BF16)],
        compiler_params=_cparams(("arbitrary", "arbitrary")),
        name="in_proj",
    )(x, sc, sh, g1.reshape(1, d), w_in, b_in.reshape(1, n), g_q.reshape(1, HEAD_DIM), g_k.reshape(1, HEAD_DIM),
      cos, sin, ln_g.reshape(1, tn), ln_b.reshape(1, tn))
    return outs


def _moba_prompt_kernel(q_ref, k_ref, v_ref, o_ref):
    s_len = q_ref.shape[1]
    nb = s_len // MOBA_BLOCK
    blk = MOBA_BLOCK
    q = q_ref[0]
    k = k_ref[0]
    means = jnp.concatenate(
        [jnp.mean(k[n * blk:(n + 1) * blk], axis=0, keepdims=True) for n in range(nb)], axis=0)
    gate = lax.dot_general(means, q, (((1,), (1,)), ((), ())), precision=HIGHEST,
                           preferred_element_type=F32)
    n_iota = lax.broadcasted_iota(I32, (nb, s_len), 0)
    own = lax.broadcasted_iota(I32, (nb, s_len), 1) // blk
    cnt = jnp.zeros((nb, s_len), I32)
    for m in range(nb):
        gm = gate[m:m + 1, :]
        beats = (m < own) & ((gm > gate) | ((gm == gate) & (m < n_iota)))
        cnt = cnt + beats.astype(I32)
    sel = ((n_iota < own) & (cnt < MOBA_TOPK)).astype(F32)
    sel = jnp.concatenate([sel, jnp.zeros((LANES - nb, s_len), F32)], axis=0)

    qb = q.astype(BF16)
    kb = k.astype(BF16)
    vb = v_ref[0].astype(BF16)
    scale = HEAD_DIM ** -0.5
    row = lax.broadcasted_iota(I32, (blk, blk), 0)
    col = lax.broadcasted_iota(I32, (blk, blk), 1)
    for i in range(nb):
        w = (i + 1) * blk
        s = lax.dot_general(qb[i * blk:(i + 1) * blk], kb[:w], (((1,), (1,)), ((), ())),
                            preferred_element_type=F32) * scale
        sel_t = sel[:, i * blk:(i + 1) * blk].T
        parts = [jnp.broadcast_to(sel_t[:, jj:jj + 1], (blk, blk)) for jj in range(i)]
        parts.append(jnp.where(col <= row, 1.0, 0.0))
        allowed = jnp.concatenate(parts, axis=1) if i else parts[0]
        s = jnp.where(allowed > 0.0, s, NEG_BIG)
        mx = jnp.max(s, axis=1, keepdims=True)
        p = jnp.exp(s - mx)
        l = jnp.sum(p, axis=1, keepdims=True)
        o = jnp.dot(p.astype(BF16), vb[:w], preferred_element_type=F32) / l
        o_ref[0, i * blk:(i + 1) * blk, :] = o.astype(o_ref.dtype)


def _moba_prompt(q, k, v):
    b, s, _ = q.shape
    assert s % MOBA_BLOCK == 0
    spec = pl.BlockSpec((1, s, HEAD_DIM), lambda bi, hi: (bi, 0, hi))
    return pl.pallas_call(
        _moba_prompt_kernel,
        grid=(b, N_HEADS),
        in_specs=[spec, spec, spec],
        out_specs=spec,
        out_shape=jax.ShapeDtypeStruct(q.shape, BF16),
        compiler_params=_cparams(("arbitrary", "arbitrary")),
        name="moba_prompt",
    )(q, k, v)


def _moba_sample_kernel(pt_ref, q_ref, kn_ref, vn_ref, kc_hbm, vc_hbm, o_ref,
                        buf, sem, qbd_scr, sc_scr, means_scr, bmax_scr, ebig_scr, selx_scr,
                        pown_scr, l_scr, out_scr, idx_vmem, idx_smem, isem, vbuf, vsem, *, n_pages, n_req):
    pg = PAGES_PER_STEP
    ns = n_pages // pg
    nblk = n_pages // 2
    n_tok = kn_ref.shape[1]
    rows = N_HEADS * n_tok
    page_rows = buf.shape[2]
    page = page_rows // N_HEADS
    r = pl.program_id(0)
    c = pl.program_id(1)
    kstep = r * ns + c
    ktotal = n_req * ns
    slot = kstep % PAGE_SLOTS
    scale = HEAD_DIM ** -0.5

    def fetch(ks_, wait):
        r_ = ks_ // ns
        c_ = ks_ % ns
        sl = ks_ % PAGE_SLOTS
        for p in range(pg):
            pidx = pt_ref[r_ * n_pages + c_ * pg + p]
            src = pl.ds(pl.multiple_of(pidx * page_rows, page_rows), page_rows)
            cp = pltpu.make_async_copy(kc_hbm.at[src, :], buf.at[sl, p], sem.at[sl, p])
            cp.wait() if wait else cp.start()

    def fetch_v(hh, wait):
        vs = hh % VBUF_SLOTS
        for t in range(n_tok):
            for s in range(MOBA_TOPK):
                blk = idx_smem[hh * n_tok + t, s]
                for half in range(2):
                    pidx = pt_ref[r * n_pages + blk * 2 + half]
                    cp = pltpu.make_async_copy(
                        vc_hbm.at[pl.ds(pl.multiple_of(pidx * page, page), page), hh, :],
                        vbuf.at[vs, t * MOBA_TOPK + s, pl.ds(half * page, page), :], vsem.at[vs])
                    cp.wait() if wait else cp.start()

    def load_page(p):
        return jnp.concatenate([buf[slot, p, pl.ds(hh, LANES, stride=N_HEADS), :] for hh in range(N_HEADS)],
                               axis=1)

    ahead = PAGE_SLOTS - 1

    @pl.when(c < ns)
    def _():
        @pl.when(kstep == 0)
        def _():
            for a in range(ahead):
                fetch(kstep + a, False)
            rb = lax.broadcasted_iota(I32, ebig_scr.shape, 0)
            cb = lax.broadcasted_iota(I32, ebig_scr.shape, 1) // LANES
            ebig_scr[...] = jnp.where(rb == cb, 1.0, 0.0).astype(BF16)

        @pl.when(kstep + ahead < ktotal)
        def _():
            fetch(kstep + ahead, False)

        fetch(kstep, True)

    @pl.when(c == 0)
    def _():
        q8 = q_ref[0]
        qrep = jnp.concatenate([q8] * N_HEADS, axis=0)
        rh = lax.broadcasted_iota(I32, qrep.shape, 0) // n_tok
        ch = lax.broadcasted_iota(I32, qrep.shape, 1) // HEAD_DIM
        qbd_scr[...] = jnp.where(rh == ch, qrep, 0.0)
        bmax_scr[...] = jnp.full(bmax_scr.shape, NEG_BIG, F32)

    @pl.when(c < ns)
    def _():
        qbd = qbd_scr[...].astype(BF16)
        lane = lax.broadcasted_iota(I32, bmax_scr.shape, 1)
        prev_sum = prev_s = None
        for p in range(pg):
            kp = load_page(p)
            s = lax.dot_general(qbd, kp.astype(BF16), (((1,), (1,)), ((), ())), preferred_element_type=F32)
            b_idx = (c * pg + p) // 2
            half = p % 2
            sc_scr[b_idx, :, half * LANES:(half + 1) * LANES] = s
            psum = jnp.sum(kp, axis=0, keepdims=True)
            if half == 0:
                prev_sum, prev_s = psum, s
            else:
                means_scr[pl.ds(b_idx, 1), :] = (prev_sum + psum) * (1.0 / MOBA_BLOCK)
                bm = jnp.max(jnp.maximum(prev_s, s), axis=1, keepdims=True)
                bmax_scr[...] = jnp.where(lane == b_idx, bm, bmax_scr[...])

    @pl.when(c == ns - 1)
    def _():
        qbd = qbd_scr[...]
        gate = lax.dot_general(qbd, means_scr[...], (((1,), (1,)), ((), ())), precision=HIGHEST,
                               preferred_element_type=F32)
        lane = lax.broadcasted_iota(I32, gate.shape, 1).astype(F32)
        pick_lane = lax.broadcasted_iota(I32, idx_vmem.shape, 1)
        g = gate
        sel = jnp.zeros(gate.shape, F32)
        picks = jnp.zeros(idx_vmem.shape, F32)
        for s in range(MOBA_TOPK):
            mx = jnp.max(g, axis=1, keepdims=True)
            idx = jnp.min(jnp.where(g == mx, lane, float(nblk)), axis=1, keepdims=True)
            pick = lane == idx
            sel = jnp.where(pick, 1.0, sel)
            g = jnp.where(pick, -jnp.inf, g)
            picks = jnp.where(pick_lane == s, idx, picks)
        idx_vmem[...] = picks.astype(I32)
        to_smem = pltpu.make_async_copy(idx_vmem, idx_smem, isem.at[0])
        to_smem.start()
        selx_scr[...] = jnp.dot(sel.astype(BF16), ebig_scr[...], preferred_element_type=F32)
        to_smem.wait()
        for hh in range(VBUF_SLOTS):
            fetch_v(hh, False)

        kn = jnp.concatenate([kn_ref[0], jnp.zeros((LANES - n_tok, kn_ref.shape[2]), F32)], axis=0)
        s_own = lax.dot_general(qbd.astype(BF16), kn.astype(BF16), (((1,), (1,)), ((), ())),
                                preferred_element_type=F32) * scale
        t_q = lax.broadcasted_iota(I32, s_own.shape, 0) % n_tok
        t_k = lax.broadcasted_iota(I32, s_own.shape, 1)
        s_own = jnp.where(t_k <= t_q, s_own, NEG_BIG)

        m_sel = jnp.max(jnp.where(sel > 0.0, bmax_scr[:, :nblk] * scale, NEG_BIG), axis=1, keepdims=True)
        m = jnp.maximum(m_sel, jnp.max(s_own, axis=1, keepdims=True))
        m_b = jnp.broadcast_to(m, (rows, LANES))

        def exp_body(b_idx, lacc):
            off = pl.multiple_of(b_idx * LANES, LANES)
            msk = selx_scr[:, pl.ds(off, LANES)] > 0.0
            s = sc_scr[b_idx]
            p0 = jnp.where(msk, jnp.exp(s[:, :LANES] * scale - m_b), 0.0)
            p1 = jnp.where(msk, jnp.exp(s[:, LANES:] * scale - m_b), 0.0)
            sc_scr[b_idx, :, :LANES] = p0
            sc_scr[b_idx, :, LANES:] = p1
            return lacc + p0 + p1

        p_own = jnp.exp(s_own - m_b)
        lacc = lax.fori_loop(0, nblk, exp_body, p_own, unroll=4)
        pown_scr[...] = p_own
        l_scr[...] = jnp.broadcast_to(jnp.sum(lacc, axis=1, keepdims=True), l_scr.shape)

    @pl.when(c >= ns)
    def _():
        hh = c - ns
        vs = hh % VBUF_SLOTS
        fetch_v(hh, True)
        row0 = pl.multiple_of(hh * n_tok, n_tok)
        col0 = pl.multiple_of(hh * HEAD_DIM, HEAD_DIM)
        vn = jnp.concatenate([vn_ref[0, :, pl.ds(col0, HEAD_DIM)], jnp.zeros((LANES - n_tok, HEAD_DIM), F32)], axis=0)
        acc = jnp.dot(pown_scr[pl.ds(row0, n_tok), :].astype(BF16), vn.astype(BF16), preferred_element_type=F32)
        tok = lax.broadcasted_iota(I32, (n_tok, MOBA_BLOCK), 0)
        for t in range(n_tok):
            for s in range(MOBA_TOPK):
                blk = idx_smem[hh * n_tok + t, s]
                w = jnp.where(tok == t, sc_scr[blk, pl.ds(row0, n_tok), :], 0.0).astype(BF16)
                acc = acc + jnp.dot(w, vbuf[vs, t * MOBA_TOPK + s].astype(BF16), preferred_element_type=F32)
        out_scr[:, pl.ds(col0, HEAD_DIM)] = acc / l_scr[pl.ds(row0, n_tok), 0:1]

        @pl.when(hh + VBUF_SLOTS < N_HEADS)
        def _():
            fetch_v(hh + VBUF_SLOTS, False)

    @pl.when(c == ns + N_HEADS - 1)
    def _():
        o_ref[0] = out_scr[...].astype(o_ref.dtype)


def _moba_sample(q, k_new, v_new, cache_k, cache_v, page_table, page):
    n_req, n_tok, width = q.shape
    n_pages = page_table.shape[1]
    assert page == LANES and 2 * page == MOBA_BLOCK and n_pages % (2 * PAGES_PER_STEP) == 0
    assert n_tok % 8 == 0 and n_tok <= LANES and n_pages // 2 <= LANES
    assert n_req * (n_pages // PAGES_PER_STEP) >= PAGE_SLOTS and n_pages // 2 >= MOBA_TOPK
    ns = n_pages // PAGES_PER_STEP
    nblk = n_pages // 2
    rows = N_HEADS * n_tok
    tok_spec = pl.BlockSpec((1, n_tok, width), lambda r, c, pt: (r, 0, 0))
    kern = functools.partial(_moba_sample_kernel, n_pages=n_pages, n_req=n_req)
    return pl.pallas_call(
        kern,
        grid_spec=pltpu.PrefetchScalarGridSpec(
            num_scalar_prefetch=1,
            grid=(n_req, ns + N_HEADS),
            in_specs=[tok_spec, tok_spec, tok_spec,
                      pl.BlockSpec(memory_space=pl.ANY), pl.BlockSpec(memory_space=pl.ANY)],
            out_specs=tok_spec,
            scratch_shapes=[pltpu.VMEM((PAGE_SLOTS, PAGES_PER_STEP, page * N_HEADS, HEAD_DIM), F32),
                            pltpu.SemaphoreType.DMA((PAGE_SLOTS, PAGES_PER_STEP)),
                            pltpu.VMEM((rows, width), F32),
                            pltpu.VMEM((nblk, rows, MOBA_BLOCK), F32),
                            pltpu.VMEM((nblk, width), F32),
                            pltpu.VMEM((rows, LANES), F32),
                            pltpu.VMEM((nblk, nblk * LANES), BF16),
                            pltpu.VMEM((rows, nblk * LANES), F32),
                            pltpu.VMEM((rows, LANES), F32),
                            pltpu.VMEM((rows, LANES), F32),
                            pltpu.VMEM((n_tok, width), F32),
                            pltpu.VMEM((rows, LANES), I32),
                            pltpu.SMEM((rows, LANES), I32),
                            pltpu.SemaphoreType.DMA((1,)),
                            pltpu.VMEM((VBUF_SLOTS, n_tok * MOBA_TOPK, MOBA_BLOCK, HEAD_DIM), F32),
                            pltpu.SemaphoreType.DMA((VBUF_SLOTS,))]),
        out_shape=jax.ShapeDtypeStruct(q.shape, BF16),
        compiler_params=_cparams(("arbitrary", "arbitrary")),
        name="moba_sample",
    )(page_table.reshape(-1), q, k_new, v_new, cache_k, cache_v)


def _mix_out_kernel(u_ref, gv_ref, sga_ref, sgb_ref, yb_ref, x_ref, gt1_ref, sc2_ref, sh2_ref, g2_ref,
                    ws_ref, bs_ref, wa_ref, wb_ref, wo_ref, wr_ref, br_ref,
                    x1_ref, h2_ref, idx_ref, gate_ref, ya_scr):
    tm = u_ref.shape[0]
    cs = ws_ref.shape[1]
    row = lax.broadcasted_iota(I32, (cs, cs), 0)
    col = lax.broadcasted_iota(I32, (cs, cs), 1)
    causal = col <= row
    bs = bs_ref[...]
    for g in range(GMLP_GROUPS):
        ws = jnp.where(causal, ws_ref[g], 0.0).astype(BF16)
        gs = slice(g * GMLP_GROUP_DIM, (g + 1) * GMLP_GROUP_DIM)
        for ci in range(tm // cs):
            rs = slice(ci * cs, (ci + 1) * cs)
            mixed = jnp.dot(ws, gv_ref[rs, gs].astype(BF16), preferred_element_type=F32) + bs[:, g:g + 1]
            ya_scr[rs, gs] = (u_ref[rs, gs].astype(F32) * mixed).astype(BF16)

    a = jnp.dot(ya_scr[...], wa_ref[...], preferred_element_type=F32)
    b = jnp.dot(yb_ref[...], wb_ref[...], preferred_element_type=F32)
    merged = sga_ref[...].astype(F32) * a + sgb_ref[...].astype(F32) * b
    mix = jnp.dot(merged.astype(BF16), wo_ref[...], preferred_element_type=F32)
    x1 = x_ref[...] + gt1_ref[0] * mix
    x1_ref[...] = x1
    r = lax.rsqrt(jnp.mean(x1 * x1, axis=-1, keepdims=True) + EPS)
    h2 = (x1 * r * g2_ref[...]) * (1.0 + sc2_ref[0]) + sh2_ref[0]
    h2_ref[...] = h2

    h_hi = h2.astype(BF16)
    h_lo = (h2 - h_hi.astype(F32)).astype(BF16)
    w_hi = wr_ref[0]
    logits = (jnp.dot(h_hi, w_hi, preferred_element_type=F32) + jnp.dot(h_lo, w_hi, preferred_element_type=F32)
              + jnp.dot(h_hi, wr_ref[1], preferred_element_type=F32)) + br_ref[...]
    lane = lax.broadcasted_iota(I32, logits.shape, 1)
    lane_f = lane.astype(F32)
    g = jnp.where(lane < N_EXPERTS, logits, -jnp.inf)
    vals, idxs = [], []
    for _ in range(TOP_K):
        mx = jnp.max(g, axis=1, keepdims=True)
        idx = jnp.min(jnp.where(g == mx, lane_f, float(LANES)), axis=1, keepdims=True)
        vals.append(mx)
        idxs.append(idx.astype(I32))
        g = jnp.where(lane_f == idx, -jnp.inf, g)
    es = [jnp.exp(v - vals[0]) for v in vals]
    den = es[0] + es[1] + es[2] + es[3]
    idx_out = jnp.zeros(logits.shape, I32)
    gate_out = jnp.zeros(logits.shape, F32)
    for kk in range(TOP_K):
        idx_out = jnp.where(lane == kk, idxs[kk], idx_out)
        gate_out = jnp.where(lane == kk, es[kk] / den, gate_out)
    idx_ref[...] = idx_out
    gate_ref[...] = gate_out


def _mix_out(u, gv, sga, sgb, yb, x, gt1, sc2, sh2, g2, ws, bs_t, w_a, w_b, w_o, w_r, b_r, tm):
    m, d = x.shape
    nb, r, _ = gt1.shape
    tiles_per_mod = m // (nb * tm)
    cs = ws.shape[1]
    assert m % tm == 0 and tm % cs == 0
    gw = u.shape[1]
    row = lambda i: (i, 0)
    const2 = lambda i: (0, 0)
    mod_spec = pl.BlockSpec((1, r, d), lambda i: (i // tiles_per_mod, 0, 0))
    resident = dict(pipeline_mode=pl.Buffered(1))
    return pl.pallas_call(
        _mix_out_kernel,
        grid=(m // tm,),
        in_specs=[pl.BlockSpec((tm, gw), row), pl.BlockSpec((tm, gw), row),
                  pl.BlockSpec((tm, d), row), pl.BlockSpec((tm, d), row),
                  pl.BlockSpec((tm, ATTN_WIDTH), row), pl.BlockSpec((tm, d), row),
                  mod_spec, mod_spec, mod_spec,
                  pl.BlockSpec((1, d), const2),
                  pl.BlockSpec(ws.shape, lambda i: (0, 0, 0), **resident),
                  pl.BlockSpec(bs_t.shape, const2),
                  pl.BlockSpec(w_a.shape, const2, **resident),
                  pl.BlockSpec(w_b.shape, const2, **resident),
                  pl.BlockSpec(w_o.shape, const2, **resident),
                  pl.BlockSpec(w_r.shape, lambda i: (0, 0, 0), **resident),
                  pl.BlockSpec((1, LANES), const2)],
        out_specs=[pl.BlockSpec((tm, d), row), pl.BlockSpec((tm, d), row),
                   pl.BlockSpec((tm, LANES), row), pl.BlockSpec((tm, LANES), row)],
        out_shape=[jax.ShapeDtypeStruct((m, d), F32), jax.ShapeDtypeStruct((m, d), F32),
                   jax.ShapeDtypeStruct((m, LANES), I32), jax.ShapeDtypeStruct((m, LANES), F32)],
        scratch_shapes=[pltpu.VMEM((tm, gw), BF16)],
        compiler_params=_cparams(("arbitrary",)),
        name="mix_out",
    )(u, gv, sga, sgb, yb, x, gt1, sc2, sh2, g2.reshape(1, d), ws, bs_t, w_a, w_b, w_o, w_r, b_r)


def _gather_kernel(src_ref, nused_ref, h_hbm, o_ref, buf, sem):
    rb = pl.program_id(0)
    tb = o_ref.shape[0]
    slot = rb % 2
    nused = nused_ref[0]

    def row_copy(blk, sl, i):
        tok = src_ref[blk * tb + i]
        return pltpu.make_async_copy(h_hbm.at[pl.ds(tok, 1), :], buf.at[sl, pl.ds(i, 1), :], sem.at[sl])

    def start_block(blk, sl):
        def body(i, carry):
            row_copy(blk, sl, i).start()
            return carry
        lax.fori_loop(0, tb, body, 0, unroll=8)

    @pl.when(rb == 0)
    def _():
        start_block(rb, slot)

    @pl.when(rb + 1 < nused)
    def _():
        start_block(rb + 1, 1 - slot)

    @pl.when(rb < nused)
    def _():
        pltpu.make_async_copy(h_hbm.at[pl.ds(0, tb), :], buf.at[slot], sem.at[slot]).wait()
        o_ref[...] = buf[slot].astype(o_ref.dtype)

    @pl.when(rb >= nused)
    def _():
        o_ref[...] = jnp.zeros(o_ref.shape, o_ref.dtype)


def _gather_rows(h2, src_tok, n_used, n_blocks):
    d = h2.shape[1]
    return pl.pallas_call(
        _gather_kernel,
        grid_spec=pltpu.PrefetchScalarGridSpec(
            num_scalar_prefetch=2,
            grid=(n_blocks,),
            in_specs=[pl.BlockSpec(memory_space=pl.ANY)],
            out_specs=pl.BlockSpec((MOE_ROWS, d), lambda rb, src, nu: (rb, 0)),
            scratch_shapes=[pltpu.VMEM((2, MOE_ROWS, d), F32), pltpu.SemaphoreType.DMA((2,))]),
        out_shape=jax.ShapeDtypeStruct((n_blocks * MOE_ROWS, d), BF16),
        compiler_params=_cparams(("arbitrary",)),
        name="moe_gather",
    )(src_tok, n_used, h2)


def _stream_weights(brun_ref, rune_ref, meta_ref, w_hbm, wbuf, sem, w_scrs, part_bases, width):
    n = pl.program_id(0)
    rb = pl.program_id(1)
    n_runs = meta_ref[1]
    run = brun_ref[rb]
    g = n * n_runs + run
    first = (rb == 0) | (brun_ref[jnp.maximum(rb - 1, 0)] != run)

    def copies(gg):
        e = rune_ref[gg % n_runs]
        nn = gg // n_runs
        sl = gg % 2
        return [pltpu.make_async_copy(
            w_hbm.at[e, :, pl.ds(pl.multiple_of(base + nn * width, width), width)], wbuf.at[sl, part], sem.at[sl, part])
            for part, base in enumerate(part_bases)]

    @pl.when(first)
    def _():
        @pl.when(g == 0)
        def _():
            for cp in copies(g):
                cp.start()

        @pl.when(g + 1 < pl.num_programs(0) * n_runs)
        def _():
            for cp in copies(g + 1):
                cp.start()

        for cp in copies(g):
            cp.wait()
        for part, scr in enumerate(w_scrs):
            scr[...] = wbuf[g % 2, part].astype(BF16)


def _up_kernel(be_ref, brun_ref, rune_ref, meta_ref, x_ref, bg_ref, bl_ref, w_hbm, o_ref,
               wbuf, sem, wg_scr, wl_scr, *, de):
    rb = pl.program_id(1)
    n_used = meta_ref[0]
    _stream_weights(brun_ref, rune_ref, meta_ref, w_hbm, wbuf, sem, (wg_scr, wl_scr), (0, de), wg_scr.shape[1])

    @pl.when(rb < n_used)
    def _():
        x = x_ref[...]
        glu = jnp.dot(x, wg_scr[...], preferred_element_type=F32) + bg_ref[0]
        lin = jnp.dot(x, wl_scr[...], preferred_element_type=F32) + bl_ref[0]
        glu = jnp.minimum(glu, SWIGLU_LIMIT)
        lin = jnp.clip(lin, -SWIGLU_LIMIT, SWIGLU_LIMIT)
        o_ref[...] = (glu * jax.nn.sigmoid(SWIGLU_ALPHA * glu) * (lin + 1.0)).astype(o_ref.dtype)

    @pl.when(rb >= n_used)
    def _():
        o_ref[...] = jnp.zeros(o_ref.shape, o_ref.dtype)


def _expert_up(xs, w1, b1, tables, hc):
    p, d = xs.shape
    de = w1.shape[2] // 2
    n_blocks = p // MOE_ROWS
    nh = de // hc
    b1r = b1.reshape(N_EXPERTS, 1, 2 * de)
    return pl.pallas_call(
        functools.partial(_up_kernel, de=de),
        grid_spec=pltpu.PrefetchScalarGridSpec(
            num_scalar_prefetch=4,
            grid=(nh, n_blocks),
            in_specs=[pl.BlockSpec((MOE_ROWS, d), lambda n, rb, be, br, re, mt: (jnp.minimum(rb, mt[0] - 1), 0)),
                      pl.BlockSpec((1, 1, hc), lambda n, rb, be, br, re, mt: (be[rb], 0, n)),
                      pl.BlockSpec((1, 1, hc), lambda n, rb, be, br, re, mt: (be[rb], 0, nh + n)),
                      pl.BlockSpec(memory_space=pl.ANY)],
            out_specs=pl.BlockSpec((MOE_ROWS, hc), lambda n, rb, be, br, re, mt: (rb, n)),
            scratch_shapes=[pltpu.VMEM((2, 2, d, hc), F32), pltpu.SemaphoreType.DMA((2, 2)),
                            pltpu.VMEM((d, hc), BF16), pltpu.VMEM((d, hc), BF16)]),
        out_shape=jax.ShapeDtypeStruct((p, de), BF16),
        compiler_params=_cparams(("arbitrary", "arbitrary")),
        name="expert_up",
    )(*tables, xs, b1r, b1r, w1)


def _down_kernel(be_ref, brun_ref, rune_ref, meta_ref, a_ref, b_ref, w_hbm, o_ref, wbuf, sem, w_scr):
    rb = pl.program_id(1)
    n_used = meta_ref[0]
    _stream_weights(brun_ref, rune_ref, meta_ref, w_hbm, wbuf, sem, (w_scr,), (0,), w_scr.shape[1])

    @pl.when(rb < n_used)
    def _():
        o_ref[...] = jnp.dot(a_ref[...], w_scr[...], preferred_element_type=F32) + b_ref[0]

    @pl.when(rb >= n_used)
    def _():
        o_ref[...] = jnp.zeros(o_ref.shape, o_ref.dtype)


def _expert_down(act, w2, b2, tables, oc):
    p, de = act.shape
    d = w2.shape[2]
    n_blocks = p // MOE_ROWS
    b2r = b2.reshape(N_EXPERTS, 1, d)
    return pl.pallas_call(
        _down_kernel,
        grid_spec=pltpu.PrefetchScalarGridSpec(
            num_scalar_prefetch=4,
            grid=(d // oc, n_blocks),
            in_specs=[pl.BlockSpec((MOE_ROWS, de), lambda n, rb, be, br, re, mt: (jnp.minimum(rb, mt[0] - 1), 0)),
                      pl.BlockSpec((1, 1, oc), lambda n, rb, be, br, re, mt: (be[rb], 0, n)),
                      pl.BlockSpec(memory_space=pl.ANY)],
            out_specs=pl.BlockSpec((MOE_ROWS, oc), lambda n, rb, be, br, re, mt: (rb, n)),
            scratch_shapes=[pltpu.VMEM((2, 1, de, oc), F32), pltpu.SemaphoreType.DMA((2, 1)),
                            pltpu.VMEM((de, oc), BF16)]),
        out_shape=jax.ShapeDtypeStruct((p, d), F32),
        compiler_params=_cparams(("arbitrary", "arbitrary")),
        name="expert_down",
    )(*tables, act, b2r, w2)


def _combine_kernel(dest_ref, y_hbm, x1_ref, gate_ref, gt2_ref, o_ref, buf, sem, *, tok0):
    i = pl.program_id(0)
    n = pl.num_programs(0)
    tm = o_ref.shape[0]
    slot = i % 2

    def row_copy(tile, sl, t, kk):
        src = dest_ref[(tok0 + tile * tm + t) * TOP_K + kk]
        return pltpu.make_async_copy(y_hbm.at[pl.ds(src, 1), :], buf.at[sl, kk, pl.ds(t, 1), :], sem.at[sl])

    def start_rows(tile, sl):
        def body(t, carry):
            for kk in range(TOP_K):
                row_copy(tile, sl, t, kk).start()
            return carry
        lax.fori_loop(0, tm, body, 0, unroll=2)

    @pl.when(i == 0)
    def _():
        start_rows(i, slot)

    @pl.when(i + 1 < n)
    def _():
        start_rows(i + 1, 1 - slot)

    for kk in range(TOP_K):
        pltpu.make_async_copy(y_hbm.at[pl.ds(0, tm), :], buf.at[slot, kk], sem.at[slot]).wait()
    gates = gate_ref[...]
    ff = gates[:, 0:1] * buf[slot, 0]
    for kk in range(1, TOP_K):
        ff = ff + gates[:, kk:kk + 1] * buf[slot, kk]
    o_ref[...] = x1_ref[...] + gt2_ref[0] * ff


def _combine(y_sorted, dest, x1, gates, gt2, tok0, tm):
    m, d = x1.shape
    nb, r, _ = gt2.shape
    tiles_per_mod = m // (nb * tm)
    kern = functools.partial(_combine_kernel, tok0=tok0)
    return pl.pallas_call(
        kern,
        grid_spec=pltpu.PrefetchScalarGridSpec(
            num_scalar_prefetch=1,
            grid=(m // tm,),
            in_specs=[pl.BlockSpec(memory_space=pl.ANY),
                      pl.BlockSpec((tm, d), lambda i, de: (i, 0)),
                      pl.BlockSpec((tm, LANES), lambda i, de: (i, 0)),
                      pl.BlockSpec((1, r, d), lambda i, de: (i // tiles_per_mod, 0, 0))],
            out_specs=pl.BlockSpec((tm, d), lambda i, de: (i, 0)),
            scratch_shapes=[pltpu.VMEM((2, TOP_K, tm, d), F32), pltpu.SemaphoreType.DMA((2,))]),
        out_shape=jax.ShapeDtypeStruct((m, d), F32),
        compiler_params=_cparams(("arbitrary",)),
        name="moe_combine",
    )(dest, y_sorted, x1, gates, gt2)


def _rope_tables(pos):
    half = HEAD_DIM // 2
    inv_freq = ROPE_THETA ** (-jnp.arange(half, dtype=F32) / half)
    ang = pos.astype(F32)[:, None] * inv_freq[None, :]
    cos = jnp.cos(ang)
    sin = jnp.sin(ang)
    return jnp.concatenate([cos, cos], axis=1), jnp.concatenate([-sin, sin], axis=1)


def _routing_tables(top_idx, n_blocks):
    e_flat = top_idx.reshape(-1)
    a = e_flat.shape[0]
    ck = LANES
    assert a % ck == 0
    onehot = e_flat[:, None] == jnp.arange(N_EXPERTS, dtype=I32)[None, :]
    within = jnp.einsum('ts,csn->ctn', jnp.tril(jnp.ones((ck, ck), BF16)), onehot.astype(BF16).reshape(a // ck, ck, -1),
                        preferred_element_type=F32)
    tot = within[:, -1, :]
    before = jnp.dot(jnp.tril(jnp.ones((a // ck, a // ck), F32), -1), tot, precision=HIGHEST)
    csum = (within + before[:, None, :]).reshape(a, N_EXPERTS).astype(I32)
    counts = csum[-1]
    rank = jnp.sum(jnp.where(onehot, csum, 0), axis=1) - 1
    padded = (counts + MOE_ROWS - 1) // MOE_ROWS * MOE_ROWS
    pad_end = jnp.cumsum(padded)
    pad_start = pad_end - padded
    dest = pad_start[e_flat] + rank
    src_tok = jnp.zeros((n_blocks * MOE_ROWS,), I32).at[dest].set(jnp.arange(a, dtype=I32) // TOP_K)
    n_used = pad_end[-1] // MOE_ROWS
    blk = jnp.arange(n_blocks, dtype=I32)
    block_e = jnp.minimum(jnp.sum((pad_end[None, :] <= blk[:, None] * MOE_ROWS).astype(I32), axis=1), N_EXPERTS - 1)
    block_e = jnp.where(blk < n_used, block_e, block_e[n_used - 1])
    first = jnp.concatenate([jnp.ones((1,), bool), block_e[1:] != block_e[:-1]])
    block_run = jnp.cumsum(first.astype(I32)) - 1
    run_e = jnp.sum(jnp.where((block_run[None, :] == jnp.arange(N_EXPERTS, dtype=I32)[:, None]) & first[None, :],
                              block_e[None, :], 0), axis=1).astype(I32)
    meta = jnp.stack([n_used, block_run[-1] + 1]).astype(I32)
    return dest.astype(I32), src_tok, (block_e, block_run.astype(I32), run_e, meta)


def kernel(x_prompt, x_sample, cache_k, cache_v, page_table, c_prompt, c_sample, w_ada, b_ada, g_norm1, g_norm2, w_in, b_in, g_q, g_k, gmlp_ln_g, gmlp_ln_b, w_spatial, b_spatial, w_branch_a, w_branch_b, w_out, w_router, b_router, w_expert_in, b_expert_in, w_expert_out, b_expert_out):
    depth = w_ada.shape[0]
    assert depth == 1
    bp, sp, d = x_prompt.shape
    bs_, ss, _ = x_sample.shape
    n_pages = page_table.shape[1]
    page = cache_k.shape[2]
    past_len = n_pages * page
    mp, ms = bp * sp, bs_ * ss
    tm_p, tm_s = 512, ms

    c_all = jnp.concatenate([c_prompt, c_sample], axis=0)
    pad = (-c_all.shape[0]) % 16
    c_all = jnp.pad(c_all, ((0, pad), (0, 0)))
    mod = _adaln(c_all, w_ada[0], b_ada[0])
    mod_p = [mod[:bp, i * d:(i + 1) * d].reshape(bp, 1, d) for i in range(6)]
    mod_s = [jnp.broadcast_to(mod[bp:bp + bs_, None, i * d:(i + 1) * d], (bs_, ss, d)).reshape(1, ms, d)
             for i in range(6)]

    w_in_b = w_in[0].astype(BF16)
    cos_p, sin_p = _rope_tables(jnp.arange(sp, dtype=I32))
    cos_s, sin_s = _rope_tables(past_len + jnp.arange(ss, dtype=I32))
    cos_s, sin_s = jnp.tile(cos_s, (bs_, 1)), jnp.tile(sin_s, (bs_, 1))

    xp = x_prompt.reshape(mp, d)
    xs = x_sample.reshape(ms, d)
    proj = functools.partial(_in_proj, g1=g_norm1[0], w_in=w_in_b, b_in=b_in[0], g_q=g_q[0], g_k=g_k[0],
                             ln_g=gmlp_ln_g[0], ln_b=gmlp_ln_b[0])
    qp, kp, vp, up, gvp, sgap, sgbp = proj(xp, mod_p[1], mod_p[0], cos=cos_p, sin=sin_p, tm=tm_p)
    qs, ks, vs, us, gvs, sgas, sgbs = proj(xs, mod_s[1], mod_s[0], cos=cos_s, sin=sin_s, tm=tm_s)

    yb_p = _moba_prompt(qp.reshape(bp, sp, ATTN_WIDTH), kp.reshape(bp, sp, ATTN_WIDTH),
                        vp.reshape(bp, sp, ATTN_WIDTH)).reshape(mp, ATTN_WIDTH)
    yb_s = _moba_sample(qs.reshape(bs_, ss, ATTN_WIDTH), ks.reshape(bs_, ss, ATTN_WIDTH),
                        vs.reshape(bs_, ss, ATTN_WIDTH), cache_k.reshape(-1, HEAD_DIM),
                        cache_v.reshape(-1, N_HEADS, HEAD_DIM), page_table, page).reshape(ms, ATTN_WIDTH)

    ws_p = w_spatial[0]
    bs_p = jnp.transpose(b_spatial[0])
    tile_t = (jnp.arange(ms, dtype=I32)[:, None] % ss == jnp.arange(ss, dtype=I32)[None, :]).astype(F32)
    same_req = jnp.arange(ms, dtype=I32)[:, None] // ss == jnp.arange(ms, dtype=I32)[None, :] // ss
    ws_s = jnp.where(same_req[None], jnp.einsum('it,gts,js->gij', tile_t, w_spatial[0][:, :ss, :ss], tile_t,
                                                precision=HIGHEST), 0.0)
    bs_s = jnp.tile(bs_p[:ss], (bs_, 1))
    w_a_b, w_b_b, w_o_b = w_branch_a[0].astype(BF16), w_branch_b[0].astype(BF16), w_out[0].astype(BF16)
    w_r_pad = jnp.pad(w_router[0], ((0, 0), (0, LANES - N_EXPERTS)))
    w_r_hi = w_r_pad.astype(BF16)
    w_r_pad = jnp.stack([w_r_hi, (w_r_pad - w_r_hi.astype(F32)).astype(BF16)])
    b_r_pad = jnp.pad(b_router[0], (0, LANES - N_EXPERTS)).reshape(1, LANES)
    mix = functools.partial(_mix_out, g2=g_norm2[0], w_a=w_a_b, w_b=w_b_b, w_o=w_o_b, w_r=w_r_pad, b_r=b_r_pad)
    n_tok = mp + ms
    x1p, h2p, idxp, gatep = mix(up, gvp, sgap, sgbp, yb_p, xp, mod_p[2], mod_p[4], mod_p[3],
                                ws=ws_p, bs_t=bs_p, tm=256)
    x1s, h2s, idxs, gates = mix(us, gvs, sgas, sgbs, yb_s, xs, mod_s[2], mod_s[4], mod_s[3],
                                ws=ws_s, bs_t=bs_s, tm=ms)

    h2 = jnp.concatenate([h2p, h2s], axis=0)
    top_idx = jnp.concatenate([idxp[:, :TOP_K], idxs[:, :TOP_K]], axis=0)
    n_assign = n_tok * TOP_K
    n_blocks = (n_assign + N_EXPERTS * (MOE_ROWS - 1) + MOE_ROWS - 1) // MOE_ROWS
    dest, src_tok, tables = _routing_tables(top_idx, n_blocks)
    x_sorted = _gather_rows(h2, src_tok, tables[3][:1], n_blocks)
    act = _expert_up(x_sorted, w_expert_in[0], b_expert_in[0], tables, hc=1024)
    y_sorted = _expert_down(act, w_expert_out[0], b_expert_out[0], tables, oc=2048)
    yp = _combine(y_sorted, dest, x1p, gatep, mod_p[5], tok0=0, tm=256)
    ys = _combine(y_sorted, dest, x1s, gates, mod_s[5], tok0=mp, tm=ms)

    shp = (depth, bp, sp, N_HEADS, HEAD_DIM)
    shs = (depth, bs_, ss, N_HEADS, HEAD_DIM)
    return (yp.reshape(bp, sp, d), ys.reshape(bs_, ss, d), kp.reshape(shp), vp.reshape(shp),
            ks.reshape(shs), vs.reshape(shs), gvs.reshape(depth, bs_, ss, gvs.shape[1]))
```

```python
import functools
import math

import jax
import jax.numpy as jnp
from jax import lax
from jax.experimental import pallas as pl
from jax.experimental.pallas import tpu as pltpu

F32 = jnp.float32
BF16 = jnp.bfloat16
I32 = jnp.int32
HIGHEST = lax.Precision.HIGHEST

N_HEADS = 8
HEAD_DIM = 128
ATTN_WIDTH = N_HEADS * HEAD_DIM
MOBA_BLOCK = 256
MOBA_TOPK = 3
ROPE_THETA = 10000.0
GMLP_GROUPS = 8
GMLP_GROUP_DIM = 128
CHUNK = 128
N_EXPERTS = 32
TOP_K = 4
SWIGLU_ALPHA = 1.702
SWIGLU_LIMIT = 7.0
EPS = 1e-6
NEG_BIG = -1e30

LANES = 128
MXU_COLS = 256
MOE_ROWS = 256
PAGES_PER_STEP = 16
PAGE_SLOTS = 3
VBUF_SLOTS = 3
DMA_PRIORITIES = 2
VMEM_LIMIT = 56 * 1024 * 1024


def _cparams(sem):
    return pltpu.CompilerParams(dimension_semantics=sem, vmem_limit_bytes=VMEM_LIMIT)


def _ada_kernel(c_ref, w_ref, b_ref, o_ref):
    c = c_ref[...]
    s = (c * jax.nn.sigmoid(c)).astype(BF16)
    o_ref[...] = jnp.dot(s, w_ref[...].astype(BF16), preferred_element_type=F32) + b_ref[...]


def _adaln(c, w_ada, b_ada):
    m, d = c.shape
    n = w_ada.shape[1]
    tn = 1024
    return pl.pallas_call(
        _ada_kernel,
        grid=(n // tn,),
        in_specs=[pl.BlockSpec((m, d), lambda j: (0, 0)),
                  pl.BlockSpec((d, tn), lambda j: (0, j)),
                  pl.BlockSpec((1, tn), lambda j: (0, j))],
        out_specs=pl.BlockSpec((m, tn), lambda j: (0, j)),
        out_shape=jax.ShapeDtypeStruct((m, n), F32),
        compiler_params=_cparams(("arbitrary",)),
        name="adaln",
    )(c, w_ada, b_ada.reshape(1, n))


def _gelu(x):
    return 0.5 * x * (1.0 + lax.erf(x * (1.0 / math.sqrt(2.0))))


def _in_kernel(x_ref, sc_ref, sh_ref, g1_ref, w_ref, b_ref, gq_ref, gk_ref, cos_ref, sin_ref, lng_ref, lnb_ref,
               q_ref, k_ref, v_ref, u_ref, gv_ref, sga_ref, sgb_ref, h_scr):
    j = pl.program_id(1)

    @pl.when(j == 0)
    def _():
        x = x_ref[...]
        r = lax.rsqrt(jnp.mean(x * x, axis=-1, keepdims=True) + EPS)
        h = (x * r * g1_ref[...]) * (1.0 + sc_ref[0]) + sh_ref[0]
        h_scr[...] = h.astype(BF16)

    tn = w_ref.shape[1]
    chunks = [slice(c0, c0 + MXU_COLS) for c0 in range(0, tn, MXU_COLS)]

    def tile(cs):
        return jnp.dot(h_scr[...], w_ref[:, cs], preferred_element_type=F32) + b_ref[:, cs]

    def qk_post(g_ref, o_ref):
        cos = cos_ref[...]
        sin = sin_ref[...]
        g = g_ref[...]
        for cs in chunks:
            z = tile(cs)
            for hh in range(MXU_COLS // HEAD_DIM):
                zh = z[:, hh * HEAD_DIM:(hh + 1) * HEAD_DIM]
                r = lax.rsqrt(jnp.mean(zh * zh, axis=-1, keepdims=True) + EPS)
                y = zh * r * g
                c0 = cs.start + hh * HEAD_DIM
                o_ref[:, c0:c0 + HEAD_DIM] = y * cos + pltpu.roll(y, HEAD_DIM // 2, 1) * sin

    @pl.when(j == 0)
    def _():
        qk_post(gq_ref, q_ref)

    @pl.when(j == 1)
    def _():
        qk_post(gk_ref, k_ref)

    @pl.when(j == 2)
    def _():
        for cs in chunks:
            v_ref[:, cs] = tile(cs)

    @pl.when(j == 3)
    def _():
        for cs in chunks:
            u_ref[:, cs] = _gelu(tile(cs)).astype(u_ref.dtype)

    @pl.when(j == 4)
    def _():
        total = None
        for cs in chunks:
            a = _gelu(tile(cs))
            gv_ref[:, cs] = a
            part = jnp.sum(a, axis=-1, keepdims=True)
            total = part if total is None else total + part
        mu = total * (1.0 / tn)
        d = gv_ref[...] - mu
        var = jnp.mean(d * d, axis=-1, keepdims=True)
        gv_ref[...] = d * lax.rsqrt(var + EPS) * lng_ref[...] + lnb_ref[...]

    @pl.when((j == 5) | (j == 6))
    def _():
        for cs in chunks:
            sga_ref[:, cs] = jax.nn.sigmoid(tile(cs)).astype(sga_ref.dtype)

    @pl.when(j >= 7)
    def _():
        for cs in chunks:
            sgb_ref[:, cs] = jax.nn.sigmoid(tile(cs)).astype(sgb_ref.dtype)


def _in_proj(x, sc, sh, g1, w_in, b_in, g_q, g_k, cos, sin, ln_g, ln_b, tm):
    m, d = x.shape
    n = w_in.shape[1]
    tn = 1024
    assert n == 9 * tn and m % tm == 0
    nb, r, _ = sc.shape
    tiles_per_mod = m // (nb * tm)
    tiles_per_tab = cos.shape[0] // tm
    row = lambda i, j: (i, 0)
    mod_spec = pl.BlockSpec((1, r, d), lambda i, j: (i // tiles_per_mod, 0, 0))
    vec_d = pl.BlockSpec((1, d), lambda i, j: (0, 0))
    vec_h = pl.BlockSpec((1, HEAD_DIM), lambda i, j: (0, 0))
    vec_n = pl.BlockSpec((1, tn), lambda i, j: (0, 0))
    tab = pl.BlockSpec((tm, HEAD_DIM), lambda i, j: (i % tiles_per_tab, 0))
    outs = pl.pallas_call(
        _in_kernel,
        grid=(m // tm, n // tn),
        in_specs=[pl.BlockSpec((tm, d), row), mod_spec, mod_spec, vec_d,
                  pl.BlockSpec((d, tn), lambda i, j: (0, j)),
                  pl.BlockSpec((1, tn), lambda i, j: (0, j)),
                  vec_h, vec_h, tab, tab, vec_n, vec_n],
        out_specs=[pl.BlockSpec((tm, tn), row), pl.BlockSpec((tm, tn), row), pl.BlockSpec((tm, tn), row),
                   pl.BlockSpec((tm, tn), row), pl.BlockSpec((tm, tn), row),
                   pl.BlockSpec((tm, tn), lambda i, j: (i, jnp.clip(j - 5, 0, 1))),
                   pl.BlockSpec((tm, tn), lambda i, j: (i, jnp.clip(j - 7, 0, 1)))],
        out_shape=[jax.ShapeDtypeStruct((m, tn), F32), jax.ShapeDtypeStruct((m, tn), F32),
                   jax.ShapeDtypeStruct((m, tn), F32), jax.ShapeDtypeStruct((m, tn), BF16),
                   jax.ShapeDtypeStruct((m, tn), F32), jax.ShapeDtypeStruct((m, 2 * tn), BF16),
                   jax.ShapeDtypeStruct((m, 2 * tn), BF16)],
        scratch_shapes=[pltpu.VMEM((tm, d), BF16)],
        compiler_params=_cparams(("arbitrary", "arbitrary")),
        name="in_proj",
    )(x, sc, sh, g1.reshape(1, d), w_in, b_in.reshape(1, n), g_q.reshape(1, HEAD_DIM), g_k.reshape(1, HEAD_DIM),
      cos, sin, ln_g.reshape(1, tn), ln_b.reshape(1, tn))
    return outs


def _moba_prompt_kernel(q_ref, k_ref, v_ref, o_ref):
    s_len = q_ref.shape[1]
    nb = s_len // MOBA_BLOCK
    blk = MOBA_BLOCK
    q = q_ref[0]
    k = k_ref[0]
    means = jnp.concatenate(
        [jnp.mean(k[n * blk:(n + 1) * blk], axis=0, keepdims=True) for n in range(nb)], axis=0)
    gate = lax.dot_general(means, q, (((1,), (1,)), ((), ())), precision=HIGHEST,
                           preferred_element_type=F32)
    n_iota = lax.broadcasted_iota(I32, (nb, s_len), 0)
    own = lax.broadcasted_iota(I32, (nb, s_len), 1) // blk
    cnt = jnp.zeros((nb, s_len), I32)
    for m in range(nb):
        gm = gate[m:m + 1, :]
        beats = (m < own) & ((gm > gate) | ((gm == gate) & (m < n_iota)))
        cnt = cnt + beats.astype(I32)
    sel = ((n_iota < own) & (cnt < MOBA_TOPK)).astype(F32)
    sel = jnp.concatenate([sel, jnp.zeros((LANES - nb, s_len), F32)], axis=0)

    qb = q.astype(BF16)
    kb = k.astype(BF16)
    vb = v_ref[0].astype(BF16)
    scale = HEAD_DIM ** -0.5
    row = lax.broadcasted_iota(I32, (blk, blk), 0)
    col = lax.broadcasted_iota(I32, (blk, blk), 1)
    for i in range(nb):
        w = (i + 1) * blk
        s = lax.dot_general(qb[i * blk:(i + 1) * blk], kb[:w], (((1,), (1,)), ((), ())),
                            preferred_element_type=F32) * scale
        sel_t = sel[:, i * blk:(i + 1) * blk].T
        parts = [jnp.broadcast_to(sel_t[:, jj:jj + 1], (blk, blk)) for jj in range(i)]
        parts.append(jnp.where(col <= row, 1.0, 0.0))
        allowed = jnp.concatenate(parts, axis=1) if i else parts[0]
        s = jnp.where(allowed > 0.0, s, NEG_BIG)
        mx = jnp.max(s, axis=1, keepdims=True)
        p = jnp.exp(s - mx)
        l = jnp.sum(p, axis=1, keepdims=True)
        o = jnp.dot(p.astype(BF16), vb[:w], preferred_element_type=F32) / l
        o_ref[0, i * blk:(i + 1) * blk, :] = o.astype(o_ref.dtype)


def _moba_prompt(q, k, v):
    b, s, _ = q.shape
    assert s % MOBA_BLOCK == 0
    spec = pl.BlockSpec((1, s, HEAD_DIM), lambda bi, hi: (bi, 0, hi))
    return pl.pallas_call(
        _moba_prompt_kernel,
        grid=(b, N_HEADS),
        in_specs=[spec, spec, spec],
        out_specs=spec,
        out_shape=jax.ShapeDtypeStruct(q.shape, BF16),
        compiler_params=_cparams(("arbitrary", "arbitrary")),
        name="moba_prompt",
    )(q, k, v)


def _moba_sample_kernel(pt_ref, q_ref, kn_ref, vn_ref, kc_hbm, vc_hbm, o_ref,
                        buf, sem, qbd_scr, sc_scr, means_scr, bmax_scr, ebig_scr, selx_scr,
                        pown_scr, l_scr, out_scr, idx_vmem, idx_smem, isem, vbuf, vsem, *, n_pages, n_req):
    pg = PAGES_PER_STEP
    ns = n_pages // pg
    nblk = n_pages // 2
    n_tok = kn_ref.shape[1]
    rows = N_HEADS * n_tok
    page_rows = buf.shape[2]
    page = page_rows // N_HEADS
    r = pl.program_id(0)
    c = pl.program_id(1)
    kstep = r * ns + c
    ktotal = n_req * ns
    slot = kstep % PAGE_SLOTS
    scale = HEAD_DIM ** -0.5

    def fetch(ks_, wait):
        r_ = ks_ // ns
        c_ = ks_ % ns
        sl = ks_ % PAGE_SLOTS
        for p in range(pg):
            pidx = pt_ref[r_ * n_pages + c_ * pg + p]
            src = pl.ds(pl.multiple_of(pidx * page_rows, page_rows), page_rows)
            cp = pltpu.make_async_copy(kc_hbm.at[src, :], buf.at[sl, p], sem.at[sl, p])
            cp.wait() if wait else cp.start()

    def fetch_v(hh, wait):
        vs = hh % VBUF_SLOTS
        for t in range(n_tok):
            for s in range(MOBA_TOPK):
                blk = idx_smem[hh * n_tok + t, s]
                for half in range(2):
                    pidx = pt_ref[r * n_pages + blk * 2 + half]
                    cp = pltpu.make_async_copy(
                        vc_hbm.at[pl.ds(pl.multiple_of(pidx * page, page), page), hh, :],
                        vbuf.at[vs, t * MOBA_TOPK + s, pl.ds(half * page, page), :], vsem.at[vs])
                    cp.wait() if wait else cp.start()

    def load_page(p):
        return jnp.concatenate([buf[slot, p, pl.ds(hh, LANES, stride=N_HEADS), :] for hh in range(N_HEADS)],
                               axis=1)

    ahead = PAGE_SLOTS - 1

    @pl.when(c < ns)
    def _():
        @pl.when(kstep == 0)
        def _():
            for a in range(ahead):
                fetch(kstep + a, False)
            rb = lax.broadcasted_iota(I32, ebig_scr.shape, 0)
            cb = lax.broadcasted_iota(I32, ebig_scr.shape, 1) // LANES
            ebig_scr[...] = jnp.where(rb == cb, 1.0, 0.0).astype(BF16)

        @pl.when(kstep + ahead < ktotal)
        def _():
            fetch(kstep + ahead, False)

        fetch(kstep, True)

    @pl.when(c == 0)
    def _():
        q8 = q_ref[0]
        qrep = jnp.concatenate([q8] * N_HEADS, axis=0)
        rh = lax.broadcasted_iota(I32, qrep.shape, 0) // n_tok
        ch = lax.broadcasted_iota(I32, qrep.shape, 1) // HEAD_DIM
        qbd_scr[...] = jnp.where(rh == ch, qrep, 0.0)
        bmax_scr[...] = jnp.full(bmax_scr.shape, NEG_BIG, F32)

    @pl.when(c < ns)
    def _():
        qbd = qbd_scr[...].astype(BF16)
        lane = lax.broadcasted_iota(I32, bmax_scr.shape, 1)
        prev_sum = prev_s = None
        for p in range(pg):
            kp = load_page(p)
            s = lax.dot_general(qbd, kp.astype(BF16), (((1,), (1,)), ((), ())), preferred_element_type=F32)
            b_idx = (c * pg + p) // 2
            half = p % 2
            sc_scr[b_idx, :, half * LANES:(half + 1) * LANES] = s
            psum = jnp.sum(kp, axis=0, keepdims=True)
            if half == 0:
                prev_sum, prev_s = psum, s
            else:
                means_scr[pl.ds(b_idx, 1), :] = (prev_sum + psum) * (1.0 / MOBA_BLOCK)
                bm = jnp.max(jnp.maximum(prev_s, s), axis=1, keepdims=True)
                bmax_scr[...] = jnp.where(lane == b_idx, bm, bmax_scr[...])

    @pl.when(c == ns - 1)
    def _():
        qbd = qbd_scr[...]
        gate = lax.dot_general(qbd, means_scr[...], (((1,), (1,)), ((), ())), precision=HIGHEST,
                               preferred_element_type=F32)
        lane = lax.broadcasted_iota(I32, gate.shape, 1).astype(F32)
        pick_lane = lax.broadcasted_iota(I32, idx_vmem.shape, 1)
        g = gate
        sel = jnp.zeros(gate.shape, F32)
        picks = jnp.zeros(idx_vmem.shape, F32)
        for s in range(MOBA_TOPK):
            mx = jnp.max(g, axis=1, keepdims=True)
            idx = jnp.min(jnp.where(g == mx, lane, float(nblk)), axis=1, keepdims=True)
            pick = lane == idx
            sel = jnp.where(pick, 1.0, sel)
            g = jnp.where(pick, -jnp.inf, g)
            picks = jnp.where(pick_lane == s, idx, picks)
        idx_vmem[...] = picks.astype(I32)
        to_smem = pltpu.make_async_copy(idx_vmem, idx_smem, isem.at[0])
        to_smem.start()
        selx_scr[...] = jnp.dot(sel.astype(BF16), ebig_scr[...], preferred_element_type=F32)
        to_smem.wait()
        for hh in range(VBUF_SLOTS):
            fetch_v(hh, False)

        kn = jnp.concatenate([kn_ref[0], jnp.zeros((LANES - n_tok, kn_ref.shape[2]), F32)], axis=0)
        s_own = lax.dot_general(qbd.astype(BF16), kn.astype(BF16), (((1,), (1,)), ((), ())),
                                preferred_element_type=F32) * scale
        t_q = lax.broadcasted_iota(I32, s_own.shape, 0) % n_tok
        t_k = lax.broadcasted_iota(I32, s_own.shape, 1)
        s_own = jnp.where(t_k <= t_q, s_own, NEG_BIG)

        m_sel = jnp.max(jnp.where(sel > 0.0, bmax_scr[:, :nblk] * scale, NEG_BIG), axis=1, keepdims=True)
        m = jnp.maximum(m_sel, jnp.max(s_own, axis=1, keepdims=True))
        m_b = jnp.broadcast_to(m, (rows, LANES))

        def exp_body(b_idx, lacc):
            off = pl.multiple_of(b_idx * LANES, LANES)
            msk = selx_scr[:, pl.ds(off, LANES)] > 0.0
            s = sc_scr[b_idx]
            p0 = jnp.where(msk, jnp.exp(s[:, :LANES] * scale - m_b), 0.0)
            p1 = jnp.where(msk, jnp.exp(s[:, LANES:] * scale - m_b), 0.0)
            sc_scr[b_idx, :, :LANES] = p0
            sc_scr[b_idx, :, LANES:] = p1
            return lacc + p0 + p1

        p_own = jnp.exp(s_own - m_b)
        lacc = lax.fori_loop(0, nblk, exp_body, p_own, unroll=4)
        pown_scr[...] = p_own
        l_scr[...] = jnp.broadcast_to(jnp.sum(lacc, axis=1, keepdims=True), l_scr.shape)

    @pl.when(c >= ns)
    def _():
        hh = c - ns
        vs = hh % VBUF_SLOTS
        fetch_v(hh, True)
        row0 = pl.multiple_of(hh * n_tok, n_tok)
        col0 = pl.multiple_of(hh * HEAD_DIM, HEAD_DIM)
        vn = jnp.concatenate([vn_ref[0, :, pl.ds(col0, HEAD_DIM)], jnp.zeros((LANES - n_tok, HEAD_DIM), F32)], axis=0)
        acc = jnp.dot(pown_scr[pl.ds(row0, n_tok), :].astype(BF16), vn.astype(BF16), preferred_element_type=F32)
        tok = lax.broadcasted_iota(I32, (n_tok, MOBA_BLOCK), 0)
        for t in range(n_tok):
            for s in range(MOBA_TOPK):
                blk = idx_smem[hh * n_tok + t, s]
                w = jnp.where(tok == t, sc_scr[blk, pl.ds(row0, n_tok), :], 0.0).astype(BF16)
                acc = acc + jnp.dot(w, vbuf[vs, t * MOBA_TOPK + s].astype(BF16), preferred_element_type=F32)
        out_scr[:, pl.ds(col0, HEAD_DIM)] = acc / l_scr[pl.ds(row0, n_tok), 0:1]

        @pl.when(hh + VBUF_SLOTS < N_HEADS)
        def _():
            fetch_v(hh + VBUF_SLOTS, False)

    @pl.when(c == ns + N_HEADS - 1)
    def _():
        o_ref[0] = out_scr[...].astype(o_ref.dtype)


def _moba_sample(q, k_new, v_new, cache_k, cache_v, page_table, page):
    n_req, n_tok, width = q.shape
    n_pages = page_table.shape[1]
    assert page == LANES and 2 * page == MOBA_BLOCK and n_pages % (2 * PAGES_PER_STEP) == 0
    assert n_tok % 8 == 0 and n_tok <= LANES and n_pages // 2 <= LANES
    assert n_req * (n_pages // PAGES_PER_STEP) >= PAGE_SLOTS and n_pages // 2 >= MOBA_TOPK
    ns = n_pages // PAGES_PER_STEP
    nblk = n_pages // 2
    rows = N_HEADS * n_tok
    tok_spec = pl.BlockSpec((1, n_tok, width), lambda r, c, pt: (r, 0, 0))
    kern = functools.partial(_moba_sample_kernel, n_pages=n_pages, n_req=n_req)
    return pl.pallas_call(
        kern,
        grid_spec=pltpu.PrefetchScalarGridSpec(
            num_scalar_prefetch=1,
            grid=(n_req, ns + N_HEADS),
            in_specs=[tok_spec, tok_spec, tok_spec,
                      pl.BlockSpec(memory_space=pl.ANY), pl.BlockSpec(memory_space=pl.ANY)],
            out_specs=tok_spec,
            scratch_shapes=[pltpu.VMEM((PAGE_SLOTS, PAGES_PER_STEP, page * N_HEADS, HEAD_DIM), F32),
                            pltpu.SemaphoreType.DMA((PAGE_SLOTS, PAGES_PER_STEP)),
                            pltpu.VMEM((rows, width), F32),
                            pltpu.VMEM((nblk, rows, MOBA_BLOCK), F32),
                            pltpu.VMEM((nblk, width), F32),
                            pltpu.VMEM((rows, LANES), F32),
                            pltpu.VMEM((nblk, nblk * LANES), BF16),
                            pltpu.VMEM((rows, nblk * LANES), F32),
                            pltpu.VMEM((rows, LANES), F32),
                            pltpu.VMEM((rows, LANES), F32),
                            pltpu.VMEM((n_tok, width), F32),
                            pltpu.VMEM((rows, LANES), I32),
                            pltpu.SMEM((rows, LANES), I32),
                            pltpu.SemaphoreType.DMA((1,)),
                            pltpu.VMEM((VBUF_SLOTS, n_tok * MOBA_TOPK, MOBA_BLOCK, HEAD_DIM), F32),
                            pltpu.SemaphoreType.DMA((VBUF_SLOTS,))]),
        out_shape=jax.ShapeDtypeStruct(q.shape, BF16),
        compiler_params=_cparams(("arbitrary", "arbitrary")),
        name="moba_sample",
    )(page_table.reshape(-1), q, k_new, v_new, cache_k, cache_v)


def _mix_out_kernel(u_ref, gv_ref, sga_ref, sgb_ref, yb_ref, x_ref, gt1_ref, sc2_ref, sh2_ref, g2_ref,
                    ws_ref, bs_ref, wa_ref, wb_ref, wo_ref, wr_ref, br_ref,
                    x1_ref, h2_ref, idx_ref, gate_ref, ya_scr):
    tm = u_ref.shape[0]
    cs = ws_ref.shape[1]
    row = lax.broadcasted_iota(I32, (cs, cs), 0)
    col = lax.broadcasted_iota(I32, (cs, cs), 1)
    causal = col <= row
    bs = bs_ref[...]
    for g in range(GMLP_GROUPS):
        ws = jnp.where(causal, ws_ref[g], 0.0).astype(BF16)
        gs = slice(g * GMLP_GROUP_DIM, (g + 1) * GMLP_GROUP_DIM)
        for ci in range(tm // cs):
            rs = slice(ci * cs, (ci + 1) * cs)
            mixed = jnp.dot(ws, gv_ref[rs, gs].astype(BF16), preferred_element_type=F32) + bs[:, g:g + 1]
            ya_scr[rs, gs] = (u_ref[rs, gs].astype(F32) * mixed).astype(BF16)

    a = jnp.dot(ya_scr[...], wa_ref[...], preferred_element_type=F32)
    b = jnp.dot(yb_ref[...], wb_ref[...], preferred_element_type=F32)
    merged = sga_ref[...].astype(F32) * a + sgb_ref[...].astype(F32) * b
    mix = jnp.dot(merged.astype(BF16), wo_ref[...], preferred_element_type=F32)
    x1 = x_ref[...] + gt1_ref[0] * mix
    x1_ref[...] = x1
    r = lax.rsqrt(jnp.mean(x1 * x1, axis=-1, keepdims=True) + EPS)
    h2 = (x1 * r * g2_ref[...]) * (1.0 + sc2_ref[0]) + sh2_ref[0]
    h2_ref[...] = h2

    h_hi = h2.astype(BF16)
    h_lo = (h2 - h_hi.astype(F32)).astype(BF16)
    w_hi = wr_ref[0]
    logits = (jnp.dot(h_hi, w_hi, preferred_element_type=F32) + jnp.dot(h_lo, w_hi, preferred_element_type=F32)
              + jnp.dot(h_hi, wr_ref[1], preferred_element_type=F32)) + br_ref[...]
    lane = lax.broadcasted_iota(I32, logits.shape, 1)
    lane_f = lane.astype(F32)
    g = jnp.where(lane < N_EXPERTS, logits, -jnp.inf)
    vals, idxs = [], []
    for _ in range(TOP_K):
        mx = jnp.max(g, axis=1, keepdims=True)
        idx = jnp.min(jnp.where(g == mx, lane_f, float(LANES)), axis=1, keepdims=True)
        vals.append(mx)
        idxs.append(idx.astype(I32))
        g = jnp.where(lane_f == idx, -jnp.inf, g)
    es = [jnp.exp(v - vals[0]) for v in vals]
    den = es[0] + es[1] + es[2] + es[3]
    idx_out = jnp.zeros(logits.shape, I32)
    gate_out = jnp.zeros(logits.shape, F32)
    for kk in range(TOP_K):
        idx_out = jnp.where(lane == kk, idxs[kk], idx_out)
        gate_out = jnp.where(lane == kk, es[kk] / den, gate_out)
    idx_ref[...] = idx_out
    gate_ref[...] = gate_out


def _mix_out(u, gv, sga, sgb, yb, x, gt1, sc2, sh2, g2, ws, bs_t, w_a, w_b, w_o, w_r, b_r, tm):
    m, d = x.shape
    nb, r, _ = gt1.shape
    tiles_per_mod = m // (nb * tm)
    cs = ws.shape[1]
    assert m % tm == 0 and tm % cs == 0
    gw = u.shape[1]
    row = lambda i: (i, 0)
    const2 = lambda i: (0, 0)
    mod_spec = pl.BlockSpec((1, r, d), lambda i: (i // tiles_per_mod, 0, 0))
    resident = dict(pipeline_mode=pl.Buffered(1))
    return pl.pallas_call(
        _mix_out_kernel,
        grid=(m // tm,),
        in_specs=[pl.BlockSpec((tm, gw), row), pl.BlockSpec((tm, gw), row),
                  pl.BlockSpec((tm, d), row), pl.BlockSpec((tm, d), row),
                  pl.BlockSpec((tm, ATTN_WIDTH), row), pl.BlockSpec((tm, d), row),
                  mod_spec, mod_spec, mod_spec,
                  pl.BlockSpec((1, d), const2),
                  pl.BlockSpec(ws.shape, lambda i: (0, 0, 0), **resident),
                  pl.BlockSpec(bs_t.shape, const2),
                  pl.BlockSpec(w_a.shape, const2, **resident),
                  pl.BlockSpec(w_b.shape, const2, **resident),
                  pl.BlockSpec(w_o.shape, const2, **resident),
                  pl.BlockSpec(w_r.shape, lambda i: (0, 0, 0), **resident),
                  pl.BlockSpec((1, LANES), const2)],
        out_specs=[pl.BlockSpec((tm, d), row), pl.BlockSpec((tm, d), row),
                   pl.BlockSpec((tm, LANES), row), pl.BlockSpec((tm, LANES), row)],
        out_shape=[jax.ShapeDtypeStruct((m, d), F32), jax.ShapeDtypeStruct((m, d), F32),
                   jax.ShapeDtypeStruct((m, LANES), I32), jax.ShapeDtypeStruct((m, LANES), F32)],
        scratch_shapes=[pltpu.VMEM((tm, gw), BF16)],
        compiler_params=_cparams(("arbitrary",)),
        name="mix_out",
    )(u, gv, sga, sgb, yb, x, gt1, sc2, sh2, g2.reshape(1, d), ws, bs_t, w_a, w_b, w_o, w_r, b_r)


def _gather_kernel(src_ref, nused_ref, h_hbm, o_ref, buf, sem):
    rb = pl.program_id(0)
    tb = o_ref.shape[0]
    slot = rb % 2
    nused = nused_ref[0]

    def row_copy(blk, sl, i):
        tok = src_ref[blk * tb + i]
        return pltpu.make_async_copy(h_hbm.at[pl.ds(tok, 1), :], buf.at[sl, pl.ds(i, 1), :], sem.at[sl])

    def start_block(blk, sl):
        def body(i, carry):
            for pr in range(DMA_PRIORITIES):
                row_copy(blk, sl, i * DMA_PRIORITIES + pr).start(priority=pr)
            return carry
        lax.fori_loop(0, tb // DMA_PRIORITIES, body, 0, unroll=4)

    @pl.when(rb == 0)
    def _():
        start_block(rb, slot)

    @pl.when(rb + 1 < nused)
    def _():
        start_block(rb + 1, 1 - slot)

    @pl.when(rb < nused)
    def _():
        pltpu.make_async_copy(h_hbm.at[pl.ds(0, tb), :], buf.at[slot], sem.at[slot]).wait()
        o_ref[...] = buf[slot].astype(o_ref.dtype)

    @pl.when(rb >= nused)
    def _():
        o_ref[...] = jnp.zeros(o_ref.shape, o_ref.dtype)


def _gather_rows(h2, src_tok, n_used, n_blocks):
    d = h2.shape[1]
    return pl.pallas_call(
        _gather_kernel,
        grid_spec=pltpu.PrefetchScalarGridSpec(
            num_scalar_prefetch=2,
            grid=(n_blocks,),
            in_specs=[pl.BlockSpec(memory_space=pl.ANY)],
            out_specs=pl.BlockSpec((MOE_ROWS, d), lambda rb, src, nu: (rb, 0)),
            scratch_shapes=[pltpu.VMEM((2, MOE_ROWS, d), F32), pltpu.SemaphoreType.DMA((2,))]),
        out_shape=jax.ShapeDtypeStruct((n_blocks * MOE_ROWS, d), BF16),
        compiler_params=_cparams(("arbitrary",)),
        name="moe_gather",
    )(src_tok, n_used, h2)


def _stream_weights(brun_ref, rune_ref, meta_ref, w_hbm, wbuf, sem, w_scrs, part_bases, width):
    n = pl.program_id(0)
    rb = pl.program_id(1)
    n_runs = meta_ref[1]
    run = brun_ref[rb]
    g = n * n_runs + run
    first = (rb == 0) | (brun_ref[jnp.maximum(rb - 1, 0)] != run)

    def copies(gg):
        e = rune_ref[gg % n_runs]
        nn = gg // n_runs
        sl = gg % 2
        return [pltpu.make_async_copy(
            w_hbm.at[e, :, pl.ds(pl.multiple_of(base + nn * width, width), width)], wbuf.at[sl, part], sem.at[sl, part])
            for part, base in enumerate(part_bases)]

    @pl.when(first)
    def _():
        @pl.when(g == 0)
        def _():
            for cp in copies(g):
                cp.start()

        @pl.when(g + 1 < pl.num_programs(0) * n_runs)
        def _():
            for cp in copies(g + 1):
                cp.start()

        for cp in copies(g):
            cp.wait()
        for part, scr in enumerate(w_scrs):
            scr[...] = wbuf[g % 2, part].astype(BF16)


def _up_kernel(be_ref, brun_ref, rune_ref, meta_ref, x_ref, bg_ref, bl_ref, w_hbm, o_ref,
               wbuf, sem, wg_scr, wl_scr, *, de):
    rb = pl.program_id(1)
    n_used = meta_ref[0]
    _stream_weights(brun_ref, rune_ref, meta_ref, w_hbm, wbuf, sem, (wg_scr, wl_scr), (0, de), wg_scr.shape[1])

    @pl.when(rb < n_used)
    def _():
        x = x_ref[...]
        glu = jnp.dot(x, wg_scr[...], preferred_element_type=F32) + bg_ref[0]
        lin = jnp.dot(x, wl_scr[...], preferred_element_type=F32) + bl_ref[0]
        glu = jnp.minimum(glu, SWIGLU_LIMIT)
        lin = jnp.clip(lin, -SWIGLU_LIMIT, SWIGLU_LIMIT)
        o_ref[...] = (glu * jax.nn.sigmoid(SWIGLU_ALPHA * glu) * (lin + 1.0)).astype(o_ref.dtype)

    @pl.when(rb >= n_used)
    def _():
        o_ref[...] = jnp.zeros(o_ref.shape, o_ref.dtype)


def _expert_up(xs, w1, b1, tables, hc):
    p, d = xs.shape
    de = w1.shape[2] // 2
    n_blocks = p // MOE_ROWS
    nh = de // hc
    b1r = b1.reshape(N_EXPERTS, 1, 2 * de)
    return pl.pallas_call(
        functools.partial(_up_kernel, de=de),
        grid_spec=pltpu.PrefetchScalarGridSpec(
            num_scalar_prefetch=4,
            grid=(nh, n_blocks),
            in_specs=[pl.BlockSpec((MOE_ROWS, d), lambda n, rb, be, br, re, mt: (jnp.minimum(rb, mt[0] - 1), 0)),
                      pl.BlockSpec((1, 1, hc), lambda n, rb, be, br, re, mt: (be[rb], 0, n)),
                      pl.BlockSpec((1, 1, hc), lambda n, rb, be, br, re, mt: (be[rb], 0, nh + n)),
                      pl.BlockSpec(memory_space=pl.ANY)],
            out_specs=pl.BlockSpec((MOE_ROWS, hc), lambda n, rb, be, br, re, mt: (rb, n)),
            scratch_shapes=[pltpu.VMEM((2, 2, d, hc), F32), pltpu.SemaphoreType.DMA((2, 2)),
                            pltpu.VMEM((d, hc), BF16), pltpu.VMEM((d, hc), BF16)]),
        out_shape=jax.ShapeDtypeStruct((p, de), BF16),
        compiler_params=_cparams(("arbitrary", "arbitrary")),
        name="expert_up",
    )(*tables, xs, b1r, b1r, w1)


def _down_kernel(be_ref, brun_ref, rune_ref, meta_ref, a_ref, b_ref, w_hbm, o_ref, wbuf, sem, w_scr):
    rb = pl.program_id(1)
    n_used = meta_ref[0]
    _stream_weights(brun_ref, rune_ref, meta_ref, w_hbm, wbuf, sem, (w_scr,), (0,), w_scr.shape[1])

    @pl.when(rb < n_used)
    def _():
        o_ref[...] = jnp.dot(a_ref[...], w_scr[...], preferred_element_type=F32) + b_ref[0]

    @pl.when(rb >= n_used)
    def _():
        o_ref[...] = jnp.zeros(o_ref.shape, o_ref.dtype)


def _expert_down(act, w2, b2, tables, oc):
    p, de = act.shape
    d = w2.shape[2]
    n_blocks = p // MOE_ROWS
    b2r = b2.reshape(N_EXPERTS, 1, d)
    return pl.pallas_call(
        _down_kernel,
        grid_spec=pltpu.PrefetchScalarGridSpec(
            num_scalar_prefetch=4,
            grid=(d // oc, n_blocks),
            in_specs=[pl.BlockSpec((MOE_ROWS, de), lambda n, rb, be, br, re, mt: (jnp.minimum(rb, mt[0] - 1), 0)),
                      pl.BlockSpec((1, 1, oc), lambda n, rb, be, br, re, mt: (be[rb], 0, n)),
                      pl.BlockSpec(memory_space=pl.ANY)],
            out_specs=pl.BlockSpec((MOE_ROWS, oc), lambda n, rb, be, br, re, mt: (rb, n)),
            scratch_shapes=[pltpu.VMEM((2, 1, de, oc), F32), pltpu.SemaphoreType.DMA((2, 1)),
                            pltpu.VMEM((de, oc), BF16)]),
        out_shape=jax.ShapeDtypeStruct((p, d), F32),
        compiler_params=_cparams(("arbitrary", "arbitrary")),
        name="expert_down",
    )(*tables, act, b2r, w2)


def _combine_kernel(dest_ref, y_hbm, x1_ref, gate_ref, gt2_ref, o_ref, buf, sem, *, tok0):
    i = pl.program_id(0)
    n = pl.num_programs(0)
    tm = o_ref.shape[0]
    slot = i % 2

    def row_copy(tile, sl, t, kk):
        src = dest_ref[(tok0 + tile * tm + t) * TOP_K + kk]
        return pltpu.make_async_copy(y_hbm.at[pl.ds(src, 1), :], buf.at[sl, kk, pl.ds(t, 1), :], sem.at[sl])

    def start_rows(tile, sl):
        def body(t, carry):
            for kk in range(TOP_K):
                row_copy(tile, sl, t, kk).start(priority=kk % DMA_PRIORITIES)
            return carry
        lax.fori_loop(0, tm, body, 0, unroll=2)

    @pl.when(i == 0)
    def _():
        start_rows(i, slot)

    @pl.when(i + 1 < n)
    def _():
        start_rows(i + 1, 1 - slot)

    for kk in range(TOP_K):
        pltpu.make_async_copy(y_hbm.at[pl.ds(0, tm), :], buf.at[slot, kk], sem.at[slot]).wait()
    gates = gate_ref[...]
    ff = gates[:, 0:1] * buf[slot, 0]
    for kk in range(1, TOP_K):
        ff = ff + gates[:, kk:kk + 1] * buf[slot, kk]
    o_ref[...] = x1_ref[...] + gt2_ref[0] * ff


def _combine(y_sorted, dest, x1, gates, gt2, tok0, tm):
    m, d = x1.shape
    nb, r, _ = gt2.shape
    tiles_per_mod = m // (nb * tm)
    kern = functools.partial(_combine_kernel, tok0=tok0)
    return pl.pallas_call(
        kern,
        grid_spec=pltpu.PrefetchScalarGridSpec(
            num_scalar_prefetch=1,
            grid=(m // tm,),
            in_specs=[pl.BlockSpec(memory_space=pl.ANY),
                      pl.BlockSpec((tm, d), lambda i, de: (i, 0)),
                      pl.BlockSpec((tm, LANES), lambda i, de: (i, 0)),
                      pl.BlockSpec((1, r, d), lambda i, de: (i // tiles_per_mod, 0, 0))],
            out_specs=pl.BlockSpec((tm, d), lambda i, de: (i, 0)),
            scratch_shapes=[pltpu.VMEM((2, TOP_K, tm, d), F32), pltpu.SemaphoreType.DMA((2,))]),
        out_shape=jax.ShapeDtypeStruct((m, d), F32),
        compiler_params=_cparams(("arbitrary",)),
        name="moe_combine",
    )(dest, y_sorted, x1, gates, gt2)


def _rope_tables(pos):
    half = HEAD_DIM // 2
    inv_freq = ROPE_THETA ** (-jnp.arange(half, dtype=F32) / half)
    ang = pos.astype(F32)[:, None] * inv_freq[None, :]
    cos = jnp.cos(ang)
    sin = jnp.sin(ang)
    return jnp.concatenate([cos, cos], axis=1), jnp.concatenate([-sin, sin], axis=1)


def _routing_tables(top_idx, n_blocks):
    e_flat = top_idx.reshape(-1)
    a = e_flat.shape[0]
    ck = LANES
    assert a % ck == 0
    onehot = e_flat[:, None] == jnp.arange(N_EXPERTS, dtype=I32)[None, :]
    within = jnp.einsum('ts,csn->ctn', jnp.tril(jnp.ones((ck, ck), BF16)), onehot.astype(BF16).reshape(a // ck, ck, -1),
                        preferred_element_type=F32)
    tot = within[:, -1, :]
    before = jnp.dot(jnp.tril(jnp.ones((a // ck, a // ck), F32), -1), tot, precision=HIGHEST)
    csum = (within + before[:, None, :]).reshape(a, N_EXPERTS).astype(I32)
    counts = csum[-1]
    rank = jnp.sum(jnp.where(onehot, csum, 0), axis=1) - 1
    padded = (counts + MOE_ROWS - 1) // MOE_ROWS * MOE_ROWS
    pad_end = jnp.cumsum(padded)
    pad_start = pad_end - padded
    dest = pad_start[e_flat] + rank
    src_tok = jnp.zeros((n_blocks * MOE_ROWS,), I32).at[dest].set(jnp.arange(a, dtype=I32) // TOP_K)
    n_used = pad_end[-1] // MOE_ROWS
    blk = jnp.arange(n_blocks, dtype=I32)
    block_e = jnp.minimum(jnp.sum((pad_end[None, :] <= blk[:, None] * MOE_ROWS).astype(I32), axis=1), N_EXPERTS - 1)
    block_e = jnp.where(blk < n_used, block_e, block_e[n_used - 1])
    first = jnp.concatenate([jnp.ones((1,), bool), block_e[1:] != block_e[:-1]])
    block_run = jnp.cumsum(first.astype(I32)) - 1
    run_e = jnp.sum(jnp.where((block_run[None, :] == jnp.arange(N_EXPERTS, dtype=I32)[:, None]) & first[None, :],
                              block_e[None, :], 0), axis=1).astype(I32)
    meta = jnp.stack([n_used, block_run[-1] + 1]).astype(I32)
    return dest.astype(I32), src_tok, (block_e, block_run.astype(I32), run_e, meta)


def kernel(x_prompt, x_sample, cache_k, cache_v, page_table, c_prompt, c_sample, w_ada, b_ada, g_norm1, g_norm2, w_in, b_in, g_q, g_k, gmlp_ln_g, gmlp_ln_b, w_spatial, b_spatial, w_branch_a, w_branch_b, w_out, w_router, b_router, w_expert_in, b_expert_in, w_expert_out, b_expert_out):
    depth = w_ada.shape[0]
    assert depth == 1
    bp, sp, d = x_prompt.shape
    bs_, ss, _ = x_sample.shape
    n_pages = page_table.shape[1]
    page = cache_k.shape[2]
    past_len = n_pages * page
    mp, ms = bp * sp, bs_ * ss
    tm_p, tm_s = 512, ms

    c_all = jnp.concatenate([c_prompt, c_sample], axis=0)
    pad = (-c_all.shape[0]) % 16
    c_all = jnp.pad(c_all, ((0, pad), (0, 0)))
    mod = _adaln(c_all, w_ada[0], b_ada[0])
    mod_p = [mod[:bp, i * d:(i + 1) * d].reshape(bp, 1, d) for i in range(6)]
    mod_s = [jnp.broadcast_to(mod[bp:bp + bs_, None, i * d:(i + 1) * d], (bs_, ss, d)).reshape(1, ms, d)
             for i in range(6)]

    w_in_b = w_in[0].astype(BF16)
    cos_p, sin_p = _rope_tables(jnp.arange(sp, dtype=I32))
    cos_s, sin_s = _rope_tables(past_len + jnp.arange(ss, dtype=I32))
    cos_s, sin_s = jnp.tile(cos_s, (bs_, 1)), jnp.tile(sin_s, (bs_, 1))

    xp = x_prompt.reshape(mp, d)
    xs = x_sample.reshape(ms, d)
    proj = functools.partial(_in_proj, g1=g_norm1[0], w_in=w_in_b, b_in=b_in[0], g_q=g_q[0], g_k=g_k[0],
                             ln_g=gmlp_ln_g[0], ln_b=gmlp_ln_b[0])
    qp, kp, vp, up, gvp, sgap, sgbp = proj(xp, mod_p[1], mod_p[0], cos=cos_p, sin=sin_p, tm=tm_p)
    qs, ks, vs, us, gvs, sgas, sgbs = proj(xs, mod_s[1], mod_s[0], cos=cos_s, sin=sin_s, tm=tm_s)

    yb_p = _moba_prompt(qp.reshape(bp, sp, ATTN_WIDTH), kp.reshape(bp, sp, ATTN_WIDTH),
                        vp.reshape(bp, sp, ATTN_WIDTH)).reshape(mp, ATTN_WIDTH)
    yb_s = _moba_sample(qs.reshape(bs_, ss, ATTN_WIDTH), ks.reshape(bs_, ss, ATTN_WIDTH),
                        vs.reshape(bs_, ss, ATTN_WIDTH), cache_k.reshape(-1, HEAD_DIM),
                        cache_v.reshape(-1, N_HEADS, HEAD_DIM), page_table, page).reshape(ms, ATTN_WIDTH)

    ws_p = w_spatial[0]
    bs_p = jnp.transpose(b_spatial[0])
    tile_t = (jnp.arange(ms, dtype=I32)[:, None] % ss == jnp.arange(ss, dtype=I32)[None, :]).astype(F32)
    same_req = jnp.arange(ms, dtype=I32)[:, None] // ss == jnp.arange(ms, dtype=I32)[None, :] // ss
    ws_s = jnp.where(same_req[None], jnp.einsum('it,gts,js->gij', tile_t, w_spatial[0][:, :ss, :ss], tile_t,
                                                precision=HIGHEST), 0.0)
    bs_s = jnp.tile(bs_p[:ss], (bs_, 1))
    w_a_b, w_b_b, w_o_b = w_branch_a[0].astype(BF16), w_branch_b[0].astype(BF16), w_out[0].astype(BF16)
    w_r_pad = jnp.pad(w_router[0], ((0, 0), (0, LANES - N_EXPERTS)))
    w_r_hi = w_r_pad.astype(BF16)
    w_r_pad = jnp.stack([w_r_hi, (w_r_pad - w_r_hi.astype(F32)).astype(BF16)])
    b_r_pad = jnp.pad(b_router[0], (0, LANES - N_EXPERTS)).reshape(1, LANES)
    mix = functools.partial(_mix_out, g2=g_norm2[0], w_a=w_a_b, w_b=w_b_b, w_o=w_o_b, w_r=w_r_pad, b_r=b_r_pad)
    n_tok = mp + ms
    x1p, h2p, idxp, gatep = mix(up, gvp, sgap, sgbp, yb_p, xp, mod_p[2], mod_p[4], mod_p[3],
                                ws=ws_p, bs_t=bs_p, tm=256)
    x1s, h2s, idxs, gates = mix(us, gvs, sgas, sgbs, yb_s, xs, mod_s[2], mod_s[4], mod_s[3],
                                ws=ws_s, bs_t=bs_s, tm=ms)

    h2 = jnp.concatenate([h2p, h2s], axis=0)
    top_idx = jnp.concatenate([idxp[:, :TOP_K], idxs[:, :TOP_K]], axis=0)
    n_assign = n_tok * TOP_K
    n_blocks = (n_assign + N_EXPERTS * (MOE_ROWS - 1) + MOE_ROWS - 1) // MOE_ROWS
    dest, src_tok, tables = _routing_tables(top_idx, n_blocks)
    x_sorted = _gather_rows(h2, src_tok, tables[3][:1], n_blocks)
    act = _expert_up(x_sorted, w_expert_in[0], b_expert_in[0], tables, hc=1024)
    y_sorted = _expert_down(act, w_expert_out[0], b_expert_out[0], tables, oc=2048)
    yp = _combine(y_sorted, dest, x1p, gatep, mod_p[5], tok0=0, tm=256)
    ys = _combine(y_sorted, dest, x1s, gates, mod_s[5], tok0=mp, tm=ms)

    shp = (depth, bp, sp, N_HEADS, HEAD_DIM)
    shs = (depth, bs_, ss, N_HEADS, HEAD_DIM)
    return (yp.reshape(bp, sp, d), ys.reshape(bs_, ss, d), kp.reshape(shp), vp.reshape(shp),
            ks.reshape(shs), vs.reshape(shs), gvs.reshape(depth, bs_, ss, gvs.shape[1]))
```
